```python
import math
import jax
import jax.numpy as jnp
from jax import lax
import numpy as np

D_MODEL = 1024
BATCH = 8
SEQ = 4096
DEPTH = 2
DEC_BATCH = 128
DEC_SEQ = 4
PAST_LEN = 16384
PAGE_SIZE = 128

N_A_LAYERS = DEPTH // 2
N_B_LAYERS = DEPTH - N_A_LAYERS
DN_HEADS = 6
DN_DK = 128
DN_DV = 128
DN_QK_W = DN_HEADS * DN_DK
DN_V_W = DN_HEADS * DN_DV
CONV_W = 4
CONV_CH = 2 * DN_QK_W + DN_V_W
DELTA_CHUNK = 64
MLA_HEADS = 6
Q_LORA = 384
KV_LORA = 256
QK_NOPE = 128
QK_ROPE = 64
V_HEAD = 128
MLA_SCALE = (QK_NOPE + QK_ROPE) ** -0.5
ROPE_THETA = 10000.0
Q_BLOCK = 128
MEM_TOKENS = 256
MEM_HEADS = 4
MEM_HD = 64
MEM_W = MEM_HEADS * MEM_HD
N_GROUPS = 4
EXP_PER_GROUP = 8
N_EXPERTS = N_GROUPS * EXP_PER_GROUP
TOP_K = 2
EXPERT_FF = 256
EPS = 1e-6
IN_A = CONV_CH + DN_V_W + 2 * DN_HEADS + MEM_W
OUT_A = DN_V_W + MEM_W
IN_B = Q_LORA + MEM_W
OUT_B = MLA_HEADS * V_HEAD + MEM_W

kernel_name = "yoco_gdn_mla_hmoe_decode_step"

F32 = jnp.float32


def rms_norm(x, g):
    x32 = x.astype(F32)
    y = x32 * lax.rsqrt(jnp.mean(x32 * x32, axis=-1, keepdims=True) + EPS)
    return (y * g.astype(F32)).astype(x.dtype)


def l2_normalize(x):
    x32 = x.astype(F32)
    return x32 * lax.rsqrt(jnp.sum(x32 * x32, axis=-1, keepdims=True) + EPS)


def rope_tables(pos):
    inv = ROPE_THETA ** (-jnp.arange(0, QK_ROPE, 2, dtype=F32) / QK_ROPE)
    ang = pos.astype(F32)[:, None] * inv[None, :]
    return jnp.cos(ang), jnp.sin(ang)


def apply_rope(x, cos, sin):
    half = x.shape[-1] // 2
    x32 = x.astype(F32)
    x1, x2 = x32[..., :half], x32[..., half:]
    return jnp.concatenate([x1 * cos - x2 * sin, x2 * cos + x1 * sin], axis=-1).astype(x.dtype)


def causal_short_conv(x, buf, w):
    L = x.shape[1]
    xp = jnp.concatenate([buf.astype(x.dtype), x], axis=1)
    y = w[0] * xp[:, 0:L]
    for j in range(1, CONV_W):
        y = y + w[j] * xp[:, j:j + L]
    return jax.nn.silu(y), xp[:, L:]


def gated_delta_rule(q, k, v, g, beta, s0):
    B, L, H, DK = q.shape
    DV = v.shape[-1]
    chunk = DELTA_CHUNK if L % DELTA_CHUNK == 0 else L
    n = L // chunk

    def blk(t):
        t = t.astype(F32).reshape((B, n, chunk, H) + t.shape[3:])
        return jnp.moveaxis(t, (1, 3), (0, 2))

    qc, kc, vc, gc, bc = blk(q), blk(k), blk(v), blk(g), blk(beta)
    G = jnp.cumsum(gc, axis=-1)
    idx = jnp.arange(chunk)
    incl = idx[:, None] >= idx[None, :]
    strict = idx[:, None] > idx[None, :]
    gamma = jnp.exp(jnp.where(incl, G[..., :, None] - G[..., None, :], -jnp.inf))
    kk = jnp.einsum('nbhik,nbhjk->nbhij', kc, kc)
    a_mat = jnp.where(strict, kk * gamma * bc[..., :, None], 0.0) + jnp.eye(chunk, dtype=F32)
    rhs = jnp.concatenate([vc * bc[..., None], kc * (bc * jnp.exp(G))[..., None]], axis=-1)
    sol = lax.linalg.triangular_solve(a_mat, rhs, left_side=True, lower=True, unit_diagonal=True)
    u, w = sol[..., :DV], sol[..., DV:]
    q_dec = qc * jnp.exp(G)[..., None]
    qk = jnp.einsum('nbhik,nbhjk->nbhij', qc, kc) * gamma
    k_dec = kc * jnp.exp(G[..., -1:] - G)[..., None]
    chunk_decay = jnp.exp(G[..., -1])

    def step(S, xs):
        q_i, qk_i, u_i, w_i, k_i, d_i = xs
        e = u_i - jnp.einsum('bhck,bhkv->bhcv', w_i, S)
        o = jnp.einsum('bhck,bhkv->bhcv', q_i, S) + jnp.einsum('bhij,bhjv->bhiv', qk_i, e)
        S = S * d_i[..., None, None] + jnp.einsum('bhck,bhcv->bhkv', k_i, e)
        return S, o

    s_fin, o = lax.scan(step, s0.astype(F32), (q_dec, qk, u, w, k_dec, chunk_decay))
    o = jnp.moveaxis(o, (0, 2), (1, 3)).reshape(B, L, H, DV)
    return o, s_fin


def memory_kv(mem, g, wk, wv):
    B, M, _ = mem.shape
    m = rms_norm(mem, g)
    return ((m @ wk).reshape(B, M, MEM_HEADS, MEM_HD), (m @ wv).reshape(B, M, MEM_HEADS, MEM_HD))


def memory_attention(q, mk, mv):
    s = jnp.einsum('blhd,bmhd->bhlm', q, mk, preferred_element_type=F32) * MEM_HD ** -0.5
    p = jax.nn.softmax(s, axis=-1).astype(mv.dtype)
    return jnp.einsum('bhlm,bmhd->blhd', p, mv)


def deltanet_mixer(u, conv_buf, s0, mem_k, mem_v, w_in, conv_w, a_log, dt_bias, g_onorm, w_out):
    B, L, _ = u.shape
    proj = u @ w_in
    qkv, z, a, b, qm = jnp.split(proj, [CONV_CH, CONV_CH + DN_V_W, CONV_CH + DN_V_W + DN_HEADS,
                                        CONV_CH + DN_V_W + 2 * DN_HEADS], axis=-1)
    qkv, new_buf = causal_short_conv(qkv, conv_buf, conv_w)
    q, k, v = jnp.split(qkv, [DN_QK_W, 2 * DN_QK_W], axis=-1)
    q = l2_normalize(q.reshape(B, L, DN_HEADS, DN_DK)) * DN_DK ** -0.5
    k = l2_normalize(k.reshape(B, L, DN_HEADS, DN_DK))
    v = v.reshape(B, L, DN_HEADS, DN_DV)
    g = -jnp.exp(a_log.astype(F32)) * jax.nn.softplus(a.astype(F32) + dt_bias.astype(F32))
    beta = jax.nn.sigmoid(b.astype(F32))
    o, s_new = gated_delta_rule(q, k, v, g, beta, s0)
    o = rms_norm(o, g_onorm) * jax.nn.silu(z.reshape(B, L, DN_HEADS, DN_DV).astype(F32))
    mo = memory_attention(qm.reshape(B, L, MEM_HEADS, MEM_HD), mem_k, mem_v)
    merged = jnp.concatenate([o.reshape(B, L, DN_V_W).astype(u.dtype), mo.reshape(B, L, MEM_W)], axis=-1)
    return merged @ w_out, new_buf, s_new.astype(s0.dtype)


def shared_latent_kv(h, g_in, w_dkv, g_lat, cos, sin):
    ckr = rms_norm(h, g_in) @ w_dkv
    c = rms_norm(ckr[..., :KV_LORA], g_lat)
    kr = apply_rope(ckr[..., KV_LORA:], cos, sin)
    return c, kr


def mla_scores(q_lat, q_rope, c, kr):
    s = jnp.einsum('bqhc,bkc->bhqk', q_lat, c, preferred_element_type=F32)
    s = s + jnp.einsum('bqhr,bkr->bhqk', q_rope, kr, preferred_element_type=F32)
    return s * MLA_SCALE


def mla_attend_prompt(q_lat, q_rope, c, kr):
    B, L, H, C = q_lat.shape
    nb = L // Q_BLOCK

    def to_blocks(t):
        return jnp.moveaxis(t.reshape(B, nb, Q_BLOCK, H, t.shape[-1]), 1, 0)

    k_pos = jnp.arange(L)

    def block(args):
        ql, qr, start = args
        s = mla_scores(ql, qr, c, kr)
        q_pos = start + jnp.arange(Q_BLOCK)
        s = jnp.where(k_pos[None, :] <= q_pos[:, None], s, -jnp.inf)
        p = jax.nn.softmax(s, axis=-1).astype(c.dtype)
        return jnp.einsum('bhqk,bkc->bqhc', p, c)

    out = lax.map(block, (to_blocks(q_lat), to_blocks(q_rope), jnp.arange(nb) * Q_BLOCK))
    return jnp.moveaxis(out, 0, 1).reshape(B, L, H, C)


def mla_attend_sample(q_lat, q_rope, c_new, kr_new, cache_lat, cache_kr, page_table):
    B, Q = q_lat.shape[:2]
    past_c = cache_lat[page_table].reshape(B, -1, KV_LORA)
    past_r = cache_kr[page_table].reshape(B, -1, QK_ROPE)
    P = past_c.shape[1]
    s_past = mla_scores(q_lat, q_rope, past_c, past_r)
    s_new = mla_scores(q_lat, q_rope, c_new, kr_new)
    causal = jnp.arange(Q)[None, :] <= jnp.arange(Q)[:, None]
    s_new = jnp.where(causal, s_new, -jnp.inf)
    p = jax.nn.softmax(jnp.concatenate([s_past, s_new], axis=-1), axis=-1)
    return (jnp.einsum('bhqk,bkc->bqhc', p[..., :P].astype(past_c.dtype), past_c)
            + jnp.einsum('bhqk,bkc->bqhc', p[..., P:].astype(c_new.dtype), c_new))


def mla_mixer(u, c, kr, mem_k, mem_v, w_in, g_q, w_uq, w_uk, w_uv, w_out, cos, sin, paged):
    B, L, _ = u.shape
    proj = u @ w_in
    cq, qm = proj[..., :Q_LORA], proj[..., Q_LORA:]
    q = (rms_norm(cq, g_q) @ w_uq).reshape(B, L, MLA_HEADS, QK_NOPE + QK_ROPE)
    q_nope, q_rope = q[..., :QK_NOPE], q[..., QK_NOPE:]
    q_rope = apply_rope(q_rope, cos[:, None], sin[:, None])
    q_lat = jnp.einsum('blhn,chn->blhc', q_nope, w_uk)
    if paged is None:
        out_lat = mla_attend_prompt(q_lat, q_rope, c, kr)
    else:
        out_lat = mla_attend_sample(q_lat, q_rope, c, kr, paged[0], paged[1], paged[2])
    o = jnp.einsum('blhc,chv->blhv', out_lat, w_uv).reshape(B, L, MLA_HEADS * V_HEAD)
    mo = memory_attention(qm.reshape(B, L, MEM_HEADS, MEM_HD), mem_k, mem_v).reshape(B, L, MEM_W)
    return jnp.concatenate([o, mo], axis=-1) @ w_out


def hier_moe(x, w_rg, b_rg, w_re, b_re, w_gate, w_up, w_down):
    shp = x.shape
    t = x.reshape(-1, shp[-1])
    T = t.shape[0]
    p_group = jax.nn.softmax(jnp.dot(t, w_rg, preferred_element_type=F32) + b_rg.astype(F32), axis=-1)
    g_val, g_idx = lax.top_k(p_group, 1)
    e_logit = (jnp.dot(t, w_re, preferred_element_type=F32) + b_re.astype(F32)).reshape(T, N_GROUPS, EXP_PER_GROUP)
    sel = jnp.broadcast_to(g_idx[:, :, None], (T, 1, EXP_PER_GROUP))
    e_in = jnp.take_along_axis(e_logit, sel, axis=1)[:, 0]
    e_val, e_idx = lax.top_k(e_in, TOP_K)
    gate = g_val * jax.nn.softmax(e_val, axis=-1)
    expert_id = g_idx * EXP_PER_GROUP + e_idx
    combine = jnp.sum(jax.nn.one_hot(expert_id, N_EXPERTS, dtype=F32) * gate[..., None], axis=1)

    def expert(acc, xs):
        wg, wu, wd, cw = xs
        h = jax.nn.silu(t @ wg) * (t @ wu)
        return acc + (h * cw[:, None]) @ wd, None

    y, _ = lax.scan(expert, jnp.zeros_like(t), (w_gate, w_up, w_down, combine.T.astype(t.dtype)))
    return y.reshape(shp)


def setup_inputs(seed: int = 0) -> dict:
    key = jax.random.key(seed)
    ks = iter(jax.random.split(key, 48))

    def nrm(shape, scale):
        return jax.random.normal(next(ks), shape, F32) * scale

    n_pages = PAST_LEN // PAGE_SIZE
    n_phys = (DEC_BATCH * n_pages * 5) // 4
    d = {}
    d["x_prompt"] = nrm((BATCH, SEQ, D_MODEL), 1.0)
    d["x_sample"] = nrm((DEC_BATCH, DEC_SEQ, D_MODEL), 1.0)
    d["mem_prompt"] = nrm((BATCH, MEM_TOKENS, D_MODEL), 1.0)
    d["state_delta"] = nrm((N_A_LAYERS, DEC_BATCH, DN_HEADS, DN_DK, DN_DV), 0.5)
    d["cache_conv"] = nrm((N_A_LAYERS, DEC_BATCH, CONV_W - 1, CONV_CH), 1.0)
    d["cache_kv_latent"] = nrm((n_phys, PAGE_SIZE, KV_LORA), 1.0)
    d["cache_k_rope"] = nrm((n_phys, PAGE_SIZE, QK_ROPE), 1.0)
    d["cache_mem_k"] = nrm((DEPTH, DEC_BATCH, MEM_TOKENS, MEM_HEADS, MEM_HD), 1.0)
    d["cache_mem_v"] = nrm((DEPTH, DEC_BATCH, MEM_TOKENS, MEM_HEADS, MEM_HD), 1.0)
    perm = jax.random.permutation(next(ks), n_phys)
    d["page_table"] = perm[:DEC_BATCH * n_pages].reshape(DEC_BATCH, n_pages).astype(jnp.int32)
    d["g_mix"] = 1.0 + nrm((DEPTH, D_MODEL), 0.05)
    d["g_ffn"] = 1.0 + nrm((DEPTH, D_MODEL), 0.05)
    d["g_final"] = 1.0 + nrm((D_MODEL,), 0.05)
    d["w_in_a"] = nrm((N_A_LAYERS, D_MODEL, IN_A), D_MODEL ** -0.5)
    d["conv_w"] = nrm((N_A_LAYERS, CONV_W, CONV_CH), 0.5)
    d["a_log"] = jnp.log(jax.random.uniform(next(ks), (N_A_LAYERS, DN_HEADS), F32, 1.0, 16.0))
    dt = jnp.exp(jax.random.uniform(next(ks), (N_A_LAYERS, DN_HEADS), F32, math.log(1e-3), math.log(1e-1)))
    d["dt_bias"] = dt + jnp.log(-jnp.expm1(-dt))
    d["g_onorm"] = 1.0 + nrm((N_A_LAYERS, DN_DV), 0.05)
    d["w_out_a"] = nrm((N_A_LAYERS, OUT_A, D_MODEL), OUT_A ** -0.5)
    d["g_kv_in"] = 1.0 + nrm((D_MODEL,), 0.05)
    d["w_dkv"] = nrm((D_MODEL, KV_LORA + QK_ROPE), D_MODEL ** -0.5)
    d["g_kv_latent"] = 1.0 + nrm((KV_LORA,), 0.05)
    d["w_uk"] = nrm((KV_LORA, MLA_HEADS, QK_NOPE), KV_LORA ** -0.5)
    d["w_uv"] = nrm((KV_LORA, MLA_HEADS, V_HEAD), KV_LORA ** -0.5)
    d["w_in_b"] = nrm((N_B_LAYERS, D_MODEL, IN_B), D_MODEL ** -0.5)
    d["g_q"] = 1.0 + nrm((N_B_LAYERS, Q_LORA), 0.05)
    d["w_uq"] = nrm((N_B_LAYERS, Q_LORA, MLA_HEADS * (QK_NOPE + QK_ROPE)), Q_LORA ** -0.5)
    d["w_out_b"] = nrm((N_B_LAYERS, OUT_B, D_MODEL), OUT_B ** -0.5)
    d["g_mem"] = 1.0 + nrm((DEPTH, D_MODEL), 0.05)
    d["w_mem_k"] = nrm((DEPTH, D_MODEL, MEM_W), D_MODEL ** -0.5)
    d["w_mem_v"] = nrm((DEPTH, D_MODEL, MEM_W), D_MODEL ** -0.5)
    d["w_router_group"] = nrm((DEPTH, D_MODEL, N_GROUPS), D_MODEL ** -0.5)
    d["b_router_group"] = nrm((DEPTH, N_GROUPS), 0.01)
    d["w_router_expert"] = nrm((DEPTH, D_MODEL, N_EXPERTS), D_MODEL ** -0.5)
    d["b_router_expert"] = nrm((DEPTH, N_EXPERTS), 0.01)
    d["w_exp_gate"] = nrm((DEPTH, N_EXPERTS, D_MODEL, EXPERT_FF), D_MODEL ** -0.5)
    d["w_exp_up"] = nrm((DEPTH, N_EXPERTS, D_MODEL, EXPERT_FF), D_MODEL ** -0.5)
    d["w_exp_down"] = nrm((DEPTH, N_EXPERTS, EXPERT_FF, D_MODEL), EXPERT_FF ** -0.5)
    return d


def reference(x_prompt, x_sample, mem_prompt, state_delta, cache_conv, cache_kv_latent, cache_k_rope,
              cache_mem_k, cache_mem_v, page_table, g_mix, g_ffn, g_final, w_in_a, conv_w, a_log, dt_bias,
              g_onorm, w_out_a, g_kv_in, w_dkv, g_kv_latent, w_uk, w_uv, w_in_b, g_q, w_uq, w_out_b,
              g_mem, w_mem_k, w_mem_v, w_router_group, b_router_group, w_router_expert, b_router_expert,
              w_exp_gate, w_exp_up, w_exp_down):
    past_len = page_table.shape[1] * PAGE_SIZE

    def run(x, pos, mem_k, mem_v, conv_bufs, d_states, paged):
        cos, sin = rope_tables(pos)
        new_bufs, new_states = [], []
        c_kv, k_rope = None, None
        for l in range(DEPTH):
            u = rms_norm(x, g_mix[l])
            if l < N_A_LAYERS:
                mix, buf, st = deltanet_mixer(u, conv_bufs[l], d_states[l], mem_k[l], mem_v[l], w_in_a[l],
                                              conv_w[l], a_log[l], dt_bias[l], g_onorm[l], w_out_a[l])
                new_bufs.append(buf)
                new_states.append(st)
            else:
                b = l - N_A_LAYERS
                mix = mla_mixer(u, c_kv, k_rope, mem_k[l], mem_v[l], w_in_b[b], g_q[b], w_uq[b], w_uk, w_uv,
                                w_out_b[b], cos, sin, paged)
            x = x + mix
            x = x + hier_moe(rms_norm(x, g_ffn[l]), w_router_group[l], b_router_group[l], w_router_expert[l],
                             b_router_expert[l], w_exp_gate[l], w_exp_up[l], w_exp_down[l])
            if l == N_A_LAYERS - 1:
                c_kv, k_rope = shared_latent_kv(x, g_kv_in, w_dkv, g_kv_latent, cos, sin)
        return rms_norm(x, g_final), c_kv, k_rope, jnp.stack(new_bufs), jnp.stack(new_states)

    Bp, Lp, _ = x_prompt.shape
    mkv = [memory_kv(mem_prompt, g_mem[l], w_mem_k[l], w_mem_v[l]) for l in range(DEPTH)]
    mem_k_prompt = jnp.stack([m[0] for m in mkv])
    mem_v_prompt = jnp.stack([m[1] for m in mkv])
    conv0 = jnp.zeros((N_A_LAYERS, Bp, CONV_W - 1, CONV_CH), x_prompt.dtype)
    state0 = jnp.zeros((N_A_LAYERS, Bp, DN_HEADS, DN_DK, DN_DV), x_prompt.dtype)
    y_prompt, kv_latent_prompt, k_rope_prompt, cache_conv_prompt, state_delta_prompt = run(
        x_prompt, jnp.arange(Lp), mem_k_prompt, mem_v_prompt, conv0, state0, None)

    Ls = x_sample.shape[1]
    y_sample, kv_latent_sample, k_rope_sample, cache_conv_sample, state_delta_sample = run(
        x_sample, past_len + jnp.arange(Ls), cache_mem_k, cache_mem_v, cache_conv, state_delta,
        (cache_kv_latent, cache_k_rope, page_table))

    return (y_prompt, y_sample, kv_latent_prompt, k_rope_prompt, kv_latent_sample, k_rope_sample,
            state_delta_prompt, cache_conv_prompt, state_delta_sample, cache_conv_sample,
            mem_k_prompt, mem_v_prompt)
```

```python
import functools
import math

import jax
import jax.numpy as jnp
from jax import lax
from jax.experimental import pallas as pl
from jax.experimental.pallas import tpu as pltpu

F32 = jnp.float32
BF16 = jnp.bfloat16
EPS = 1e-6
ROPE_THETA = 10000.0
TOP_K = 2
DELTA_CHUNK = 64
LANES = 128
SUBLANES = 8
VMEM_LIMIT = 56 * 1024 * 1024
NEG_INF = float("-inf")
HIGHEST = lax.Precision.HIGHEST

NT_DIMS = (((1,), (1,)), ((), ()))
TN_DIMS = (((0,), (0,)), ((), ()))


def _cparams(*sem):
    return pltpu.CompilerParams(dimension_semantics=sem, vmem_limit_bytes=VMEM_LIMIT)


def _rms(x, g):
    return x * lax.rsqrt(jnp.mean(x * x, axis=-1, keepdims=True) + EPS) * g


def _silu(x):
    return x * jax.nn.sigmoid(x)


def _dot(a, b, precision=None):
    return jnp.dot(a, b, preferred_element_type=F32, precision=precision)


def _dot_nt(a, b, precision=None):
    return lax.dot_general(a, b, NT_DIMS, preferred_element_type=F32, precision=precision)


def _token_tile(t, cap=512):
    tm = min(t, cap)
    assert t % tm == 0
    return tm


def _norm_matmul_kernel(x_ref, g_ref, w_ref, *out_refs, splits):
    xn = _rms(x_ref[...], g_ref[...]).astype(BF16)
    off = 0
    for o_ref, n in zip(out_refs, splits):
        o_ref[...] = _dot(xn, w_ref[:, off:off + n]).astype(o_ref.dtype)
        off += n


def norm_matmul(x, g, w_bf16, splits, out_dtypes):
    t, d = x.shape
    tm = _token_tile(t)
    n = w_bf16.shape[1]
    assert sum(splits) == n and all(s % LANES == 0 for s in splits)
    return pl.pallas_call(
        functools.partial(_norm_matmul_kernel, splits=tuple(splits)),
        grid=(t // tm,),
        in_specs=[pl.BlockSpec((tm, d), lambda i: (i, 0)),
                  pl.BlockSpec((1, d), lambda i: (0, 0)),
                  pl.BlockSpec((d, n), lambda i: (0, 0))],
        out_specs=[pl.BlockSpec((tm, s), lambda i: (i, 0)) for s in splits],
        out_shape=[jax.ShapeDtypeStruct((t, s), dt) for s, dt in zip(splits, out_dtypes)],
        compiler_params=_cparams("parallel"),
        name="norm_matmul",
    )(x, g.reshape(1, d), w_bf16)


def _proj_residual_kernel(a_ref, b_ref, wa_ref, wb_ref, res_ref, o_ref):
    acc = _dot(a_ref[...].astype(BF16), wa_ref[...]) + _dot(b_ref[...].astype(BF16), wb_ref[...])
    o_ref[...] = res_ref[...] + acc


def proj_residual(a, b, w_bf16, res):
    t, d = res.shape
    ka, kb = a.shape[1], b.shape[1]
    tm = _token_tile(t)
    wa, wb = w_bf16[:ka], w_bf16[ka:]
    return pl.pallas_call(
        _proj_residual_kernel,
        grid=(t // tm,),
        in_specs=[pl.BlockSpec((tm, ka), lambda i: (i, 0)),
                  pl.BlockSpec((tm, kb), lambda i: (i, 0)),
                  pl.BlockSpec((ka, d), lambda i: (0, 0)),
                  pl.BlockSpec((kb, d), lambda i: (0, 0)),
                  pl.BlockSpec((tm, d), lambda i: (i, 0))],
        out_specs=pl.BlockSpec((tm, d), lambda i: (i, 0)),
        out_shape=jax.ShapeDtypeStruct((t, d), F32),
        compiler_params=_cparams("parallel"),
        name="proj_residual",
    )(a, b, wa, wb, res)


def _delta_kernel(qkv_ref, ab_ref, z_ref, cbuf_ref, s0_ref, cw_ref, alog_ref, dtb_ref, gon_ref,
                  o_ref, nbuf_ref, sout_ref, xp_scr, s_scr, *, rows, n_valid, heads, dk, dv, n_taps):
    li = pl.program_id(1)
    halo = n_taps - 1
    base = SUBLANES
    qk_w = heads * dk

    @pl.when(li == 0)
    def _():
        s_scr[...] = s0_ref[0]
        xp_scr[base - halo:base, :] = cbuf_ref[0]

    xp_scr[base:base + rows, :] = qkv_ref[0]

    acc = cw_ref[0:1, :] * xp_scr[base - halo:base - halo + rows, :]
    for j in range(1, n_taps):
        acc = acc + cw_ref[j:j + 1, :] * xp_scr[base - halo + j:base - halo + j + rows, :]
    y = _silu(acc)

    tail = xp_scr[base + n_valid - halo:base + n_valid, :]
    xp_scr[base - halo:base, :] = tail
    nbuf_ref[0] = tail

    row = lax.broadcasted_iota(jnp.int32, (rows, rows), 0)
    col = lax.broadcasted_iota(jnp.int32, (rows, rows), 1)
    incl = row >= col
    strict = row > col
    lane = lax.broadcasted_iota(jnp.int32, (rows, LANES), 1)

    ab = ab_ref[0]
    sp = jnp.maximum(ab + dtb_ref[...], 0.0) + jnp.log1p(jnp.exp(-jnp.abs(ab + dtb_ref[...])))
    g_all = -jnp.exp(alog_ref[...]) * sp
    beta_all = jax.nn.sigmoid(ab)
    if n_valid < rows:
        valid = lax.broadcasted_iota(jnp.int32, (rows, LANES), 0) < n_valid
        g_all = jnp.where(valid, g_all, 0.0)
        beta_all = jnp.where(valid, beta_all, 0.0)

    gcum = _dot(incl.astype(F32), g_all, HIGHEST)
    glast = gcum[rows - 1:rows, :]
    exp_g = jnp.exp(gcum)
    exp_rem = jnp.exp(glast - gcum)
    exp_last = jnp.exp(glast)
    ones = jnp.ones((rows, LANES), F32)
    eye = (row == col).astype(F32)
    n_double = max(0, int(math.ceil(math.log2(rows))) - 1)
    z = z_ref[0]

    for h in range(heads):
        qh = y[:, h * dk:(h + 1) * dk]
        kh = y[:, qk_w + h * dk:qk_w + (h + 1) * dk]
        vh = y[:, 2 * qk_w + h * dv:2 * qk_w + (h + 1) * dv]
        qn = qh * lax.rsqrt(jnp.sum(qh * qh, axis=-1, keepdims=True) + EPS) * dk ** -0.5
        kn = kh * lax.rsqrt(jnp.sum(kh * kh, axis=-1, keepdims=True) + EPS)
        g_col = gcum[:, h:h + 1]
        beta = beta_all[:, heads + h:heads + h + 1]
        eg = exp_g[:, h:h + 1]
        er = exp_rem[:, h:h + 1]
        el = exp_last[:, h:h + 1]

        g_row = _dot_nt(ones, jnp.where(lane == h, gcum, 0.0), HIGHEST)
        gamma = jnp.exp(jnp.where(incl, g_col - g_row, NEG_INF))
        kb = kn.astype(BF16)
        qb = qn.astype(BF16)
        kk = _dot_nt(kb, kb)
        x = -jnp.where(strict, kk * gamma * beta, 0.0)
        tinv = eye + x
        pw = x
        for _ in range(n_double):
            pw = _dot(pw, pw, HIGHEST)
            tinv = tinv + _dot(tinv, pw, HIGHEST)
        rhs = jnp.concatenate([vh * beta, kn * (beta * eg)], axis=-1)
        sol = _dot(tinv, rhs, HIGHEST)
        u, w = sol[:, :dv], sol[:, dv:]

        s = s_scr[h]
        sb = s.astype(BF16)
        e = u - _dot(w.astype(BF16), sb)
        eb = e.astype(BF16)
        qk = _dot_nt(qb, kb) * gamma
        o = _dot((qn * eg).astype(BF16), sb) + _dot(qk.astype(BF16), eb)
        s_scr[h] = s * el + lax.dot_general((kn * er).astype(BF16), eb, TN_DIMS,
                                            preferred_element_type=F32)
        zh = z[:, h * dv:(h + 1) * dv]
        o_ref[0, :, h * dv:(h + 1) * dv] = (_rms(o, gon_ref[...]) * _silu(zh)).astype(o_ref.dtype)

    @pl.when(li == pl.num_programs(1) - 1)
    def _():
        sout_ref[0] = s_scr[...]


def delta_mixer(qkv, ab, z, conv_buf, s0, conv_w, a_log, dt_bias, g_onorm, *, rows, n_valid):
    bsz, lp, ch = qkv.shape
    _, heads, dk, dv = s0.shape
    n_taps = conv_w.shape[0]
    n_blk = lp // rows
    assert lp % rows == 0 and (n_valid == rows or n_blk == 1)
    pad_h = lambda v: jnp.pad(v.astype(F32), (0, LANES - heads)).reshape(1, LANES)
    kern = functools.partial(_delta_kernel, rows=rows, n_valid=n_valid, heads=heads, dk=dk, dv=dv,
                             n_taps=n_taps)
    return pl.pallas_call(
        kern,
        grid=(bsz, n_blk),
        in_specs=[pl.BlockSpec((1, rows, ch), lambda b, l: (b, l, 0)),
                  pl.BlockSpec((1, rows, LANES), lambda b, l: (b, l, 0)),
                  pl.BlockSpec((1, rows, heads * dv), lambda b, l: (b, l, 0)),
                  pl.BlockSpec((1, n_taps - 1, ch), lambda b, l: (b, 0, 0)),
                  pl.BlockSpec((1, heads, dk, dv), lambda b, l: (b, 0, 0, 0)),
                  pl.BlockSpec((n_taps, ch), lambda b, l: (0, 0)),
                  pl.BlockSpec((1, LANES), lambda b, l: (0, 0)),
                  pl.BlockSpec((1, LANES), lambda b, l: (0, 0)),
                  pl.BlockSpec((1, dv), lambda b, l: (0, 0))],
        out_specs=[pl.BlockSpec((1, rows, heads * dv), lambda b, l: (b, l, 0)),
                   pl.BlockSpec((1, n_taps - 1, ch), lambda b, l: (b, 0, 0)),
                   pl.BlockSpec((1, heads, dk, dv), lambda b, l: (b, 0, 0, 0))],
        out_shape=[jax.ShapeDtypeStruct((bsz, lp, heads * dv), BF16),
                   jax.ShapeDtypeStruct((bsz, n_taps - 1, ch), F32),
                   jax.ShapeDtypeStruct((bsz, heads, dk, dv), F32)],
        scratch_shapes=[pltpu.VMEM((SUBLANES + rows, ch), F32),
                        pltpu.VMEM((heads, dk, dv), F32)],
        compiler_params=_cparams("parallel", "arbitrary"),
        name="delta_mixer",
    )(qkv, ab, z, conv_buf, s0, conv_w, pad_h(a_log), pad_h(dt_bias), g_onorm.reshape(1, dv))


def _mem_attn_kernel(q_ref, k_ref, v_ref, o_ref, *, heads, hd):
    q = q_ref[0]
    k = k_ref[0].astype(BF16)
    v = v_ref[0].astype(BF16)
    lane_head = lax.broadcasted_iota(jnp.int32, q.shape, 1) // hd
    out = jnp.zeros(q.shape, F32)
    for h in range(heads):
        sel = lane_head == h
        s = _dot_nt(jnp.where(sel, q, 0.0).astype(BF16), k) * hd ** -0.5
        p = jnp.exp(s - jnp.max(s, axis=-1, keepdims=True))
        p = p / jnp.sum(p, axis=-1, keepdims=True)
        out = out + jnp.where(sel, _dot(p.astype(BF16), v), 0.0)
    o_ref[0] = out


def mem_attention(q, mk, mv, heads):
    bsz, lp, w = q.shape
    m = mk.shape[1]
    rows = min(lp, 512)
    assert lp % rows == 0
    return pl.pallas_call(
        functools.partial(_mem_attn_kernel, heads=heads, hd=w // heads),
        grid=(bsz, lp // rows),
        in_specs=[pl.BlockSpec((1, rows, w), lambda b, l: (b, l, 0)),
                  pl.BlockSpec((1, m, w), lambda b, l: (b, 0, 0)),
                  pl.BlockSpec((1, m, w), lambda b, l: (b, 0, 0))],
        out_specs=pl.BlockSpec((1, rows, w), lambda b, l: (b, l, 0)),
        out_shape=jax.ShapeDtypeStruct((bsz, lp, w), F32),
        compiler_params=_cparams("parallel", "parallel"),
        name="mem_attention",
    )(q, mk, mv)


def _router_kernel(x_ref, g_ref, w_ref, b_ref, t_ref, ids_ref, gates_ref, *, n_groups, per_group):
    t = _rms(x_ref[...], g_ref[...])
    t_ref[...] = t.astype(BF16)
    logits = _dot(t, w_ref[...], HIGHEST) + b_ref[...]
    lane = lax.broadcasted_iota(jnp.int32, logits.shape, 1)

    def first_argmax(v):
        m = jnp.max(v, axis=-1, keepdims=True)
        return m, jnp.min(jnp.where(v == m, lane, LANES), axis=-1, keepdims=True)

    gl = jnp.where(lane < n_groups, logits, NEG_INF)
    gmax, gidx = first_argmax(gl)
    g_val = 1.0 / jnp.sum(jnp.exp(gl - gmax), axis=-1, keepdims=True)
    lo = n_groups + gidx * per_group
    el = jnp.where(lane >= lo, jnp.where(lane < lo + per_group, logits, NEG_INF), NEG_INF)
    m1, i1 = first_argmax(el)
    m2, i2 = first_argmax(jnp.where(lane == i1, NEG_INF, el))
    ex = jnp.exp(m2 - m1)
    den = 1.0 + ex
    gate1 = g_val * (1.0 / den)
    gate2 = g_val * (ex / den)
    ids_ref[...] = jnp.where(lane == 0, i1 - n_groups, jnp.where(lane == 1, i2 - n_groups, 0))
    gates_ref[...] = jnp.where(lane == 0, gate1, jnp.where(lane == 1, gate2, 0.0))


def moe_router(x, g, w_rg, b_rg, w_re, b_re):
    t, d = x.shape
    n_groups, n_exp = w_rg.shape[1], w_re.shape[1]
    assert n_groups + n_exp <= LANES
    pad = LANES - n_groups - n_exp
    w = jnp.pad(jnp.concatenate([w_rg, w_re], axis=1), ((0, 0), (0, pad)))
    b = jnp.pad(jnp.concatenate([b_rg, b_re]), (0, pad)).reshape(1, LANES)
    tm = _token_tile(t)
    kern = functools.partial(_router_kernel, n_groups=n_groups, per_group=n_exp // n_groups)
    return pl.pallas_call(
        kern,
        grid=(t // tm,),
        in_specs=[pl.BlockSpec((tm, d), lambda i: (i, 0)),
                  pl.BlockSpec((1, d), lambda i: (0, 0)),
                  pl.BlockSpec((d, LANES), lambda i: (0, 0)),
                  pl.BlockSpec((1, LANES), lambda i: (0, 0))],
        out_specs=[pl.BlockSpec((tm, d), lambda i: (i, 0)),
                   pl.BlockSpec((tm, LANES), lambda i: (i, 0)),
                   pl.BlockSpec((tm, LANES), lambda i: (i, 0))],
        out_shape=[jax.ShapeDtypeStruct((t, d), BF16),
                   jax.ShapeDtypeStruct((t, LANES), jnp.int32),
                   jax.ShapeDtypeStruct((t, LANES), F32)],
        compiler_params=_cparams("parallel"),
        name="moe_router",
    )(x, g.reshape(1, d), w, b)


def _expert_kernel(te_ref, nu_ref, xs_ref, wg_ref, wu_ref, wd_ref, ys_ref):
    i = pl.program_id(0)

    @pl.when(i < nu_ref[0])
    def _():
        x = xs_ref[...]
        h = _silu(_dot(x, wg_ref[0].astype(BF16))) * _dot(x, wu_ref[0].astype(BF16))
        ys_ref[...] = _dot(h.astype(BF16), wd_ref[0].astype(BF16))

    @pl.when(i >= nu_ref[0])
    def _():
        ys_ref[...] = jnp.zeros_like(ys_ref)


def expert_ffn(tile_expert, n_used, xs, w_gate, w_up, w_down, tile):
    n_slots, d = xs.shape
    ff = w_gate.shape[-1]
    grid_spec = pltpu.PrefetchScalarGridSpec(
        num_scalar_prefetch=2,
        grid=(n_slots // tile,),
        in_specs=[pl.BlockSpec((tile, d), lambda i, te, nu: (i, 0)),
                  pl.BlockSpec((1, d, ff), lambda i, te, nu: (te[i], 0, 0)),
                  pl.BlockSpec((1, d, ff), lambda i, te, nu: (te[i], 0, 0)),
                  pl.BlockSpec((1, ff, d), lambda i, te, nu: (te[i], 0, 0))],
        out_specs=pl.BlockSpec((tile, d), lambda i, te, nu: (i, 0)),
    )
    return pl.pallas_call(
        _expert_kernel,
        grid_spec=grid_spec,
        out_shape=jax.ShapeDtypeStruct((n_slots, d), F32),
        compiler_params=_cparams("arbitrary"),
        name="expert_ffn",
    )(tile_expert, n_used, xs, w_gate, w_up, w_down)


def _combine_kernel(x_ref, ya_ref, yb_ref, gates_ref, g_ref, o_ref, *, final_norm):
    gates = gates_ref[...]
    out = x_ref[...] + (ya_ref[...] * gates[:, 0:1] + yb_ref[...] * gates[:, 1:2])
    if final_norm:
        out = _rms(out, g_ref[...])
    o_ref[...] = out


def moe_combine(x, ya, yb, gates, g_final, final_norm):
    t, d = x.shape
    tm = _token_tile(t)
    row = pl.BlockSpec((tm, d), lambda i: (i, 0))
    return pl.pallas_call(
        functools.partial(_combine_kernel, final_norm=final_norm),
        grid=(t // tm,),
        in_specs=[row, row, row, pl.BlockSpec((tm, LANES), lambda i: (i, 0)),
                  pl.BlockSpec((1, d), lambda i: (0, 0))],
        out_specs=row,
        out_shape=jax.ShapeDtypeStruct((t, d), F32),
        compiler_params=_cparams("parallel"),
        name="moe_combine",
    )(x, ya, yb, gates, g_final.reshape(1, d))


def hier_moe(x, g_ffn, w_rg, b_rg, w_re, b_re, w_gate, w_up, w_down, g_final, final_norm):
    t, d = x.shape
    n_exp = w_gate.shape[0]
    tile = 256 if t >= 8192 else 64
    tn, ids, gates = moe_router(x, g_ffn, w_rg, b_rg, w_re, b_re)
    flat = ids[:, :TOP_K].reshape(-1)
    n_pick = flat.shape[0]
    order = jnp.argsort(flat)
    sorted_e = flat[order]
    experts = jnp.arange(n_exp, dtype=jnp.int32)
    starts = jnp.searchsorted(sorted_e, experts, side="left").astype(jnp.int32)
    counts = jnp.searchsorted(sorted_e, experts, side="right").astype(jnp.int32) - starts
    padded = (counts + tile - 1) // tile * tile
    pend = jnp.cumsum(padded)
    pstart = pend - padded
    n_tiles = (n_pick + n_exp * (tile - 1)) // tile
    n_slots = n_tiles * tile
    tile_expert = jnp.minimum(
        jnp.searchsorted(pend, jnp.arange(n_tiles, dtype=jnp.int32) * tile, side="right"),
        n_exp - 1).astype(jnp.int32)
    n_used = (pend[-1:] // tile).astype(jnp.int32)
    slot = jnp.arange(n_slots, dtype=jnp.int32)
    slot_e = jnp.repeat(tile_expert, tile)
    rank = slot - pstart[slot_e]
    sorted_pos = jnp.clip(starts[slot_e] + rank, 0, n_pick - 1)
    src = jnp.where(rank < counts[slot_e], order[sorted_pos] // TOP_K, 0)
    dest = pstart[sorted_e] + jnp.arange(n_pick, dtype=jnp.int32) - starts[sorted_e]
    slot_of_pick = dest[jnp.argsort(order)].reshape(t, TOP_K)

    xs = jnp.take(tn, src, axis=0)
    ys = expert_ffn(tile_expert, n_used, xs, w_gate, w_up, w_down, tile)
    ya = jnp.take(ys, slot_of_pick[:, 0], axis=0)
    yb = jnp.take(ys, slot_of_pick[:, 1], axis=0)
    return moe_combine(x, ya, yb, gates, g_final, final_norm)


def _latent_q_kernel(x_ref, gkv_ref, wdkv_ref, glat_ref, cos_ref, sin_ref, gmix_ref, winb_ref, gq_ref,
                     wuq_ref, wuk_ref, c_ref, kr_ref, cb_ref, krb_ref, qm_ref, ql_ref, qr_ref,
                     *, heads, kv, rd, nope, q_lora, scale):
    x = x_ref[...]
    xr = x * lax.rsqrt(jnp.mean(x * x, axis=-1, keepdims=True) + EPS)
    cos = cos_ref[...]
    sin = sin_ref[...]

    ckr = _dot((xr * gkv_ref[...]).astype(BF16), wdkv_ref[...])
    c = _rms(ckr[:, :kv], glat_ref[...])
    kr = ckr[:, kv:kv + rd] * cos + ckr[:, kv + LANES:kv + LANES + rd] * sin
    c_ref[...] = c
    kr_ref[...] = kr
    cb_ref[...] = c.astype(BF16)
    krb_ref[...] = kr.astype(BF16)

    proj = _dot((xr * gmix_ref[...]).astype(BF16), winb_ref[...])
    qm_ref[...] = proj[:, q_lora:]
    cq = _rms(proj[:, :q_lora], gq_ref[...]).astype(BF16)
    q = _dot(cq, wuq_ref[...])
    hw = nope + 2 * LANES
    for h in range(heads):
        qn = q[:, h * hw:h * hw + nope].astype(BF16)
        qrope = (q[:, h * hw + nope:h * hw + nope + rd] * cos
                 + q[:, h * hw + nope + LANES:h * hw + nope + LANES + rd] * sin)
        ql_ref[h] = (_dot(qn, wuk_ref[h]) * scale).astype(BF16)
        qr_ref[h] = (qrope * scale).astype(BF16)


def _swap_halves(w):
    half = w.shape[-1] // 2
    return jnp.concatenate([w[..., half:], w[..., :half]], axis=-1)


def latent_and_queries(x, g_kv_in, w_dkv, g_lat, cos_full, sin_signed, g_mix, w_in_b, g_q, w_uq, w_uk,
                       scale, table_blocks):
    t, d = x.shape
    kv = g_lat.shape[0]
    rd = w_dkv.shape[1] - kv
    _, heads, nope = w_uk.shape
    q_lora = g_q.shape[0]
    mem_w = w_in_b.shape[1] - q_lora
    assert rd <= LANES and kv % LANES == 0 and nope % LANES == 0 and q_lora % LANES == 0
    padl = lambda w: jnp.pad(w, ((0, 0), (0, LANES - rd)))
    w_r = w_dkv[:, kv:]
    wdkv_ext = jnp.concatenate([w_dkv[:, :kv], padl(w_r), padl(_swap_halves(w_r))], axis=1).astype(BF16)
    w_uq_h = w_uq.reshape(q_lora, heads, nope + rd)
    w_qr = w_uq_h[..., nope:]
    pad3 = lambda w: jnp.pad(w, ((0, 0), (0, 0), (0, LANES - rd)))
    wuq_ext = jnp.concatenate([w_uq_h[..., :nope], pad3(w_qr), pad3(_swap_halves(w_qr))], axis=-1)
    hw = nope + 2 * LANES
    wuq_ext = wuq_ext.reshape(q_lora, heads * hw).astype(BF16)
    wuk_t = jnp.transpose(w_uk, (1, 2, 0)).astype(BF16)
    tm = _token_tile(t, 256)
    nblk = table_blocks(tm)
    kern = functools.partial(_latent_q_kernel, heads=heads, kv=kv, rd=rd, nope=nope, q_lora=q_lora,
                             scale=scale)
    full = lambda shape: pl.BlockSpec(shape, lambda i: (0,) * len(shape))
    return pl.pallas_call(
        kern,
        grid=(t // tm,),
        in_specs=[pl.BlockSpec((tm, d), lambda i: (i, 0)),
                  full((1, d)), full(wdkv_ext.shape), full((1, kv)),
                  pl.BlockSpec((tm, rd), lambda i: (i % nblk, 0)),
                  pl.BlockSpec((tm, rd), lambda i: (i % nblk, 0)),
                  full((1, d)), full(w_in_b.shape), full((1, q_lora)),
                  full(wuq_ext.shape), full(wuk_t.shape)],
        out_specs=[pl.BlockSpec((tm, kv), lambda i: (i, 0)),
                   pl.BlockSpec((tm, rd), lambda i: (i, 0)),
                   pl.BlockSpec((tm, kv), lambda i: (i, 0)),
                   pl.BlockSpec((tm, rd), lambda i: (i, 0)),
                   pl.BlockSpec((tm, mem_w), lambda i: (i, 0)),
                   pl.BlockSpec((heads, tm, kv), lambda i: (0, i, 0)),
                   pl.BlockSpec((heads, tm, rd), lambda i: (0, i, 0))],
        out_shape=[jax.ShapeDtypeStruct((t, kv), F32),
                   jax.ShapeDtypeStruct((t, rd), F32),
                   jax.ShapeDtypeStruct((t, kv), BF16),
                   jax.ShapeDtypeStruct((t, rd), BF16),
                   jax.ShapeDtypeStruct((t, mem_w), F32),
                   jax.ShapeDtypeStruct((heads, t, kv), BF16),
                   jax.ShapeDtypeStruct((heads, t, rd), BF16)],
        compiler_params=_cparams("parallel"),
        name="latent_and_queries",
    )(x, g_kv_in.reshape(1, d), wdkv_ext, g_lat.reshape(1, kv), cos_full, sin_signed,
      g_mix.reshape(1, d), w_in_b.astype(BF16), g_q.reshape(1, q_lora), wuq_ext, wuk_t)


def _mla_prompt_kernel(qi_ref, kj_ref, ql_ref, qr_ref, c_ref, kr_ref, wuv_ref, o_ref,
                       m_scr, l_scr, acc_scr, *, heads, tq, vh):
    p_idx = pl.program_id(1)
    qi = qi_ref[p_idx]
    kj = kj_ref[p_idx]
    rows = heads * tq

    @pl.when(kj == 0)
    def _():
        m_scr[...] = jnp.full_like(m_scr, NEG_INF)
        l_scr[...] = jnp.zeros_like(l_scr)
        acc_scr[...] = jnp.zeros_like(acc_scr)

    def step(masked):
        ql = ql_ref[...].reshape(rows, ql_ref.shape[-1])
        qr = qr_ref[...].reshape(rows, qr_ref.shape[-1])
        c = c_ref[...]
        s = _dot_nt(ql, c) + _dot_nt(qr, kr_ref[...])
        if masked:
            qpos = lax.broadcasted_iota(jnp.int32, (heads, tq, tq), 1).reshape(rows, tq)
            kpos = lax.broadcasted_iota(jnp.int32, (rows, tq), 1)
            s = jnp.where(kpos <= qpos, s, NEG_INF)
        m_prev = m_scr[...]
        m_new = jnp.maximum(m_prev, jnp.max(s, axis=-1, keepdims=True))
        alpha = jnp.exp(m_prev - m_new)
        p = jnp.exp(s - m_new)
        l_scr[...] = alpha * l_scr[...] + jnp.sum(p, axis=-1, keepdims=True)
        acc_scr[...] = alpha * acc_scr[...] + _dot(p.astype(BF16), c)
        m_scr[...] = m_new

    @pl.when(kj < qi)
    def _():
        step(False)

    @pl.when(kj == qi)
    def _():
        step(True)
        out = (acc_scr[...] / l_scr[...]).astype(BF16)
        for h in range(heads):
            o_ref[:, h * vh:(h + 1) * vh] = _dot(out[h * tq:(h + 1) * tq], wuv_ref[h])


def mla_prompt_attention(ql, qr, cb, krb, w_uv, bsz, seq, tq=256):
    heads, t, kv = ql.shape
    rd = qr.shape[-1]
    vh = w_uv.shape[-1]
    tq = min(tq, seq)
    assert seq % tq == 0
    nq = seq // tq
    pairs = [(i, j) for i in range(nq) for j in range(i + 1)]
    qi = jnp.asarray([p[0] for p in pairs], jnp.int32)
    kj = jnp.asarray([p[1] for p in pairs], jnp.int32)
    wuv_h = jnp.transpose(w_uv, (1, 0, 2)).astype(BF16)
    grid_spec = pltpu.PrefetchScalarGridSpec(
        num_scalar_prefetch=2,
        grid=(bsz, len(pairs)),
        in_specs=[pl.BlockSpec((heads, tq, kv), lambda b, p, qi, kj: (0, b * nq + qi[p], 0)),
                  pl.BlockSpec((heads, tq, rd), lambda b, p, qi, kj: (0, b * nq + qi[p], 0)),
                  pl.BlockSpec((tq, kv), lambda b, p, qi, kj: (b * nq + kj[p], 0)),
                  pl.BlockSpec((tq, rd), lambda b, p, qi, kj: (b * nq + kj[p], 0)),
                  pl.BlockSpec((heads, kv, vh), lambda b, p, qi, kj: (0, 0, 0))],
        out_specs=pl.BlockSpec((tq, heads * vh), lambda b, p, qi, kj: (b * nq + qi[p], 0)),
        scratch_shapes=[pltpu.VMEM((heads * tq, 1), F32),
                        pltpu.VMEM((heads * tq, 1), F32),
                        pltpu.VMEM((heads * tq, kv), F32)],
    )
    return pl.pallas_call(
        functools.partial(_mla_prompt_kernel, heads=heads, tq=tq, vh=vh),
        grid_spec=grid_spec,
        out_shape=jax.ShapeDtypeStruct((t, heads * vh), F32),
        compiler_params=_cparams("parallel", "arbitrary"),
        name="mla_prompt_attention",
    )(qi, kj, ql, qr, cb, krb, wuv_h)


def _mla_paged_kernel(pt_ref, ql_ref, qr_ref, cn_ref, krn_ref, clat_hbm, ckr_hbm, wuv_ref, o_ref,
                      kbuf, rbuf, sems, m_scr, l_scr, acc_scr,
                      *, heads, n_new, page, pages_per_chunk, chunks_per_seq, vh):
    g = pl.program_id(0)
    total = pl.num_programs(0)
    ci = g % chunks_per_seq
    slot = g % 2

    def page_copies(chunk, slot_, p):
        pg = pt_ref[chunk * pages_per_chunk + p]
        rows = pl.ds(p * page, page)
        return (pltpu.make_async_copy(clat_hbm.at[pg], kbuf.at[slot_, rows], sems.at[0, slot_]),
                pltpu.make_async_copy(ckr_hbm.at[pg], rbuf.at[slot_, rows], sems.at[1, slot_]))

    def start_chunk(chunk, slot_):
        for p in range(pages_per_chunk):
            for cp in page_copies(chunk, slot_, p):
                cp.start()

    @pl.when(g == 0)
    def _():
        start_chunk(0, 0)

    @pl.when(g + 1 < total)
    def _():
        start_chunk(g + 1, 1 - slot)

    for p in range(pages_per_chunk):
        for cp in page_copies(g, slot, p):
            cp.wait()

    @pl.when(ci == 0)
    def _():
        m_scr[...] = jnp.full_like(m_scr, NEG_INF)
        l_scr[...] = jnp.zeros_like(l_scr)
        acc_scr[...] = jnp.zeros_like(acc_scr)

    ql = ql_ref[0]
    qr = qr_ref[0]

    def update(s, v):
        m_prev = m_scr[...]
        m_new = jnp.maximum(m_prev, jnp.max(s, axis=-1, keepdims=True))
        alpha = jnp.exp(m_prev - m_new)
        p = jnp.exp(s - m_new)
        l_scr[...] = alpha * l_scr[...] + jnp.sum(p, axis=-1, keepdims=True)
        acc_scr[...] = alpha * acc_scr[...] + _dot(p.astype(BF16), v)
        m_scr[...] = m_new

    k = kbuf[slot].astype(BF16)
    update(_dot_nt(ql, k) + _dot_nt(qr, rbuf[slot].astype(BF16)), k)

    @pl.when(ci == chunks_per_seq - 1)
    def _():
        cn = cn_ref[0].astype(BF16)
        s = _dot_nt(ql, cn) + _dot_nt(qr, krn_ref[0].astype(BF16))
        qpos = lax.broadcasted_iota(jnp.int32, s.shape, 0) % n_new
        kpos = lax.broadcasted_iota(jnp.int32, s.shape, 1)
        update(jnp.where(kpos <= qpos, s, NEG_INF), cn)
        out = (acc_scr[...] / l_scr[...]).astype(BF16)
        for h in range(heads):
            o_ref[0, :, h * vh:(h + 1) * vh] = _dot(out, wuv_ref[h])


def mla_paged_attention(page_table, ql, qr, c_new, kr_new, cache_lat, cache_kr, w_uv):
    bsz, n_pages = page_table.shape
    _, page, kv = cache_lat.shape
    rd = cache_kr.shape[-1]
    q_rows = ql.shape[1]
    n_new = c_new.shape[1]
    heads, vh = w_uv.shape[1], w_uv.shape[2]
    pages_per_chunk = min(n_pages, 32)
    assert n_pages % pages_per_chunk == 0
    chunks_per_seq = n_pages // pages_per_chunk
    chunk_rows = pages_per_chunk * page
    new_pad = -n_new % SUBLANES
    c_new = jnp.pad(c_new, ((0, 0), (0, new_pad), (0, 0)))
    kr_new = jnp.pad(kr_new, ((0, 0), (0, new_pad), (0, 0)))
    wuv_h = jnp.transpose(w_uv, (1, 0, 2)).astype(BF16)
    kern = functools.partial(_mla_paged_kernel, heads=heads, n_new=n_new, page=page,
                             pages_per_chunk=pages_per_chunk, chunks_per_seq=chunks_per_seq, vh=vh)
    seq_of = lambda g, pt: (g // chunks_per_seq, 0, 0)
    grid_spec = pltpu.PrefetchScalarGridSpec(
        num_scalar_prefetch=1,
        grid=(bsz * chunks_per_seq,),
        in_specs=[pl.BlockSpec((1, q_rows, kv), seq_of),
                  pl.BlockSpec((1, q_rows, rd), seq_of),
                  pl.BlockSpec((1, n_new + new_pad, kv), seq_of),
                  pl.BlockSpec((1, n_new + new_pad, rd), seq_of),
                  pl.BlockSpec(memory_space=pl.ANY),
                  pl.BlockSpec(memory_space=pl.ANY),
                  pl.BlockSpec((heads, kv, vh), lambda g, pt: (0, 0, 0))],
        out_specs=pl.BlockSpec((1, q_rows, heads * vh), seq_of),
        scratch_shapes=[pltpu.VMEM((2, chunk_rows, kv), F32),
                        pltpu.VMEM((2, chunk_rows, rd), F32),
                        pltpu.SemaphoreType.DMA((2, 2)),
                        pltpu.VMEM((q_rows, 1), F32),
                        pltpu.VMEM((q_rows, 1), F32),
                        pltpu.VMEM((q_rows, kv), F32)],
    )
    return pl.pallas_call(
        kern,
        grid_spec=grid_spec,
        out_shape=jax.ShapeDtypeStruct((bsz, q_rows, heads * vh), F32),
        compiler_params=_cparams("arbitrary"),
        name="mla_paged_attention",
    )(page_table.reshape(-1), ql, qr, c_new, kr_new, cache_lat, cache_kr, wuv_h)


def _rope_tables(pos, rd):
    inv = ROPE_THETA ** (-jnp.arange(0, rd, 2, dtype=F32) / rd)
    ang = pos.astype(F32)[:, None] * inv[None, :]
    cos, sin = jnp.cos(ang), jnp.sin(ang)
    return jnp.concatenate([cos, cos], axis=-1), jnp.concatenate([-sin, sin], axis=-1)


def _run_group(x3, pos, mem_k, mem_v, conv_buf, d_state, paged, p):
    bsz, seq, d = x3.shape
    t = bsz * seq
    x = x3.reshape(t, d)
    heads_dn, dk, dv = d_state.shape[1:]
    conv_ch = p["conv_w"].shape[-1]
    v_w = heads_dn * dv
    mem_heads = mem_k.shape[3]
    mem_w = mem_heads * mem_k.shape[4]
    mem_tokens = mem_k.shape[2]

    w_in = p["w_in_a"][0]
    w_ab = jnp.pad(w_in[:, conv_ch + v_w:conv_ch + v_w + 2 * heads_dn], ((0, 0), (0, LANES - 2 * heads_dn)))
    w_in_r = jnp.concatenate([w_in[:, :conv_ch + v_w], w_in[:, conv_ch + v_w + 2 * heads_dn:], w_ab],
                             axis=1).astype(BF16)
    qkv, z, qm, ab = norm_matmul(x, p["g_mix"][0], w_in_r, (conv_ch, v_w, mem_w, LANES), (F32,) * 4)
    rows = DELTA_CHUNK if seq % DELTA_CHUNK == 0 else -(-seq // SUBLANES) * SUBLANES
    seq_p = -(-seq // rows) * rows
    seq3 = lambda a: jnp.pad(a.reshape(bsz, seq, -1), ((0, 0), (0, seq_p - seq), (0, 0)))
    o_dn, new_buf, new_state = delta_mixer(seq3(qkv), seq3(ab), seq3(z), conv_buf, d_state, p["conv_w"][0],
                                           p["a_log"][0], p["dt_bias"][0], p["g_onorm"][0],
                                           rows=rows, n_valid=min(seq, rows))
    o_dn = o_dn[:, :seq].reshape(t, v_w)
    mo = mem_attention(seq3(qm), mem_k[0].reshape(bsz, mem_tokens, mem_w),
                       mem_v[0].reshape(bsz, mem_tokens, mem_w), mem_heads)[:, :seq].reshape(t, mem_w)
    x = proj_residual(o_dn, mo, p["w_out_a"][0].astype(BF16), x)
    x = hier_moe(x, p["g_ffn"][0], p["w_router_group"][0], p["b_router_group"][0], p["w_router_expert"][0],
                 p["b_router_expert"][0], p["w_exp_gate"][0], p["w_exp_up"][0], p["w_exp_down"][0],
                 p["g_final"], False)

    kv = p["g_kv_latent"].shape[0]
    rd = p["w_dkv"].shape[1] - kv
    nope = p["w_uk"].shape[2]
    mla_heads = p["w_uk"].shape[1]
    scale = (nope + rd) ** -0.5
    if paged is None:
        cos_t, sin_t = _rope_tables(pos, rd)
        table_blocks = lambda tm: seq // tm
    else:
        cos_t, sin_t = _rope_tables(jnp.tile(pos, bsz), rd)
        table_blocks = lambda tm: t // tm
    c, kr, cb, krb, qm, ql, qr = latent_and_queries(
        x, p["g_kv_in"], p["w_dkv"], p["g_kv_latent"], cos_t, sin_t, p["g_mix"][1], p["w_in_b"][0],
        p["g_q"][0], p["w_uq"][0], p["w_uk"], scale, table_blocks)
    if paged is None:
        o_mla = mla_prompt_attention(ql, qr, cb, krb, p["w_uv"], bsz, seq)
    else:
        q_rows = -(-mla_heads * seq // 16) * 16
        to_rows = lambda q: jnp.pad(
            jnp.transpose(q.reshape(mla_heads, bsz, seq, -1), (1, 0, 2, 3)).reshape(bsz, mla_heads * seq, -1),
            ((0, 0), (0, q_rows - mla_heads * seq), (0, 0)))
        o_all = mla_paged_attention(paged[2], to_rows(ql), to_rows(qr), c.reshape(bsz, seq, kv),
                                    kr.reshape(bsz, seq, rd), paged[0], paged[1], p["w_uv"])
        vh = p["w_uv"].shape[2]
        o_all = o_all[:, :mla_heads * seq].reshape(bsz, mla_heads, seq, mla_heads, vh)
        o_mla = jnp.stack([o_all[:, h, :, h] for h in range(mla_heads)], axis=2).reshape(t, mla_heads * vh)
    mo = mem_attention(seq3(qm), mem_k[1].reshape(bsz, mem_tokens, mem_w),
                       mem_v[1].reshape(bsz, mem_tokens, mem_w), mem_heads)[:, :seq].reshape(t, mem_w)
    x = proj_residual(o_mla, mo, p["w_out_b"][0].astype(BF16), x)
    y = hier_moe(x, p["g_ffn"][1], p["w_router_group"][1], p["b_router_group"][1], p["w_router_expert"][1],
                 p["b_router_expert"][1], p["w_exp_gate"][1], p["w_exp_up"][1], p["w_exp_down"][1],
                 p["g_final"], True)
    return (y.reshape(bsz, seq, d), c.reshape(bsz, seq, kv), kr.reshape(bsz, seq, rd),
            new_buf[None], new_state[None])


def kernel(x_prompt, x_sample, mem_prompt, state_delta, cache_conv, cache_kv_latent, cache_k_rope, cache_mem_k, cache_mem_v, page_table, g_mix, g_ffn, g_final, w_in_a, conv_w, a_log, dt_bias, g_onorm, w_out_a, g_kv_in, w_dkv, g_kv_latent, w_uk, w_uv, w_in_b, g_q, w_uq, w_out_b, g_mem, w_mem_k, w_mem_v, w_router_group, b_router_group, w_router_expert, b_router_expert, w_exp_gate, w_exp_up, w_exp_down):
    p = dict(g_mix=g_mix, g_ffn=g_ffn, g_final=g_final, w_in_a=w_in_a, conv_w=conv_w, a_log=a_log,
             dt_bias=dt_bias, g_onorm=g_onorm, w_out_a=w_out_a, g_kv_in=g_kv_in, w_dkv=w_dkv,
             g_kv_latent=g_kv_latent, w_uk=w_uk, w_uv=w_uv, w_in_b=w_in_b, g_q=g_q, w_uq=w_uq,
             w_out_b=w_out_b, w_router_group=w_router_group, b_router_group=b_router_group,
             w_router_expert=w_router_expert, b_router_expert=b_router_expert, w_exp_gate=w_exp_gate,
             w_exp_up=w_exp_up, w_exp_down=w_exp_down)
    depth = g_mix.shape[0]
    assert depth == 2 and w_in_a.shape[0] == 1
    bp, lp, d = x_prompt.shape
    bs, ls, _ = x_sample.shape
    mem_tokens = mem_prompt.shape[1]
    mem_heads, mem_hd = cache_mem_k.shape[3], cache_mem_k.shape[4]
    mem_w = mem_heads * mem_hd

    mem_flat = mem_prompt.reshape(bp * mem_tokens, d)
    mkv = [norm_matmul(mem_flat, g_mem[l], jnp.concatenate([w_mem_k[l], w_mem_v[l]], axis=1).astype(BF16),
                       (mem_w, mem_w), (F32, F32)) for l in range(depth)]
    shape5 = (bp, mem_tokens, mem_heads, mem_hd)
    mem_k_prompt = jnp.stack([m[0].reshape(shape5) for m in mkv])
    mem_v_prompt = jnp.stack([m[1].reshape(shape5) for m in mkv])
    conv0 = jnp.zeros((bp,) + cache_conv.shape[2:], F32)
    state0 = jnp.zeros((bp,) + state_delta.shape[2:], F32)
    y_p, c_p, kr_p, conv_p, state_p = _run_group(
        x_prompt, jnp.arange(lp), mem_k_prompt, mem_v_prompt, conv0, state0, None, p)

    past_len = page_table.shape[1] * cache_kv_latent.shape[1]
    y_s, c_s, kr_s, conv_s, state_s = _run_group(
        x_sample, past_len + jnp.arange(ls), cache_mem_k, cache_mem_v, cache_conv[0], state_delta[0],
        (cache_kv_latent, cache_k_rope, page_table), p)

    return (y_p, y_s, c_p, kr_p, c_s, kr_s, state_p, conv_p, state_s, conv_s, mem_k_prompt, mem_v_prompt)
```

```python
import functools
import math

import jax
import jax.numpy as jnp
from jax import lax
from jax.experimental import pallas as pl
from jax.experimental.pallas import tpu as pltpu

F32 = jnp.float32
BF16 = jnp.bfloat16
EPS = 1e-6
ROPE_THETA = 10000.0
TOP_K = 2
DELTA_CHUNK = 64
LANES = 128
SUBLANES = 8
VMEM_LIMIT = 56 * 1024 * 1024
NEG_INF = float("-inf")
HIGHEST = lax.Precision.HIGHEST

NT_DIMS = (((1,), (1,)), ((), ()))
TN_DIMS = (((0,), (0,)), ((), ()))


def _cparams(*sem):
    return pltpu.CompilerParams(dimension_semantics=sem, vmem_limit_bytes=VMEM_LIMIT)


def _rms(x, g):
    return x * lax.rsqrt(jnp.mean(x * x, axis=-1, keepdims=True) + EPS) * g


def _silu(x):
    return x * jax.nn.sigmoid(x)


def _dot(a, b, precision=None):
    return jnp.dot(a, b, preferred_element_type=F32, precision=precision)


def _dot_nt(a, b, precision=None):
    return lax.dot_general(a, b, NT_DIMS, preferred_element_type=F32, precision=precision)


def _lane_tile(x, k):
    return jnp.concatenate([x] * k, axis=1)


def _token_tile(t, cap=512):
    tm = min(t, cap)
    assert t % tm == 0
    return tm


def _norm_matmul_kernel(x_ref, g_ref, w_ref, *out_refs, splits):
    xn = _rms(x_ref[...], g_ref[...]).astype(BF16)
    off = 0
    for o_ref, n in zip(out_refs, splits):
        o_ref[...] = _dot(xn, w_ref[:, off:off + n]).astype(o_ref.dtype)
        off += n


def norm_matmul(x, g, w_bf16, splits, out_dtypes):
    t, d = x.shape
    tm = _token_tile(t)
    n = w_bf16.shape[1]
    assert sum(splits) == n and all(s % LANES == 0 for s in splits)
    return pl.pallas_call(
        functools.partial(_norm_matmul_kernel, splits=tuple(splits)),
        grid=(t // tm,),
        in_specs=[pl.BlockSpec((tm, d), lambda i: (i, 0)),
                  pl.BlockSpec((1, d), lambda i: (0, 0)),
                  pl.BlockSpec((d, n), lambda i: (0, 0))],
        out_specs=[pl.BlockSpec((tm, s), lambda i: (i, 0)) for s in splits],
        out_shape=[jax.ShapeDtypeStruct((t, s), dt) for s, dt in zip(splits, out_dtypes)],
        compiler_params=_cparams("parallel"),
        name="norm_matmul",
    )(x, g.reshape(1, d), w_bf16)


def _proj_residual_kernel(a_ref, b_ref, wa_ref, wb_ref, res_ref, o_ref):
    acc = _dot(a_ref[...].astype(BF16), wa_ref[...]) + _dot(b_ref[...].astype(BF16), wb_ref[...])
    o_ref[...] = res_ref[...] + acc


def proj_residual(a, b, w_bf16, res):
    t, d = res.shape
    ka, kb = a.shape[1], b.shape[1]
    tm = _token_tile(t)
    wa, wb = w_bf16[:ka], w_bf16[ka:]
    return pl.pallas_call(
        _proj_residual_kernel,
        grid=(t // tm,),
        in_specs=[pl.BlockSpec((tm, ka), lambda i: (i, 0)),
                  pl.BlockSpec((tm, kb), lambda i: (i, 0)),
                  pl.BlockSpec((ka, d), lambda i: (0, 0)),
                  pl.BlockSpec((kb, d), lambda i: (0, 0)),
                  pl.BlockSpec((tm, d), lambda i: (i, 0))],
        out_specs=pl.BlockSpec((tm, d), lambda i: (i, 0)),
        out_shape=jax.ShapeDtypeStruct((t, d), F32),
        compiler_params=_cparams("parallel"),
        name="proj_residual",
    )(a, b, wa, wb, res)


def _delta_kernel(qkv_ref, ab_ref, z_ref, cbuf_ref, s0_ref, cw_ref, alog_ref, dtb_ref, gon_ref,
                  o_ref, nbuf_ref, sout_ref, xp_scr, s_scr, *, rows, n_valid, nb, heads, dk, dv, n_taps):
    li = pl.program_id(1)
    halo = n_taps - 1
    base = SUBLANES
    qk_w = heads * dk

    @pl.when(li == 0)
    def _():
        s_scr[...] = s0_ref[...]
        xp_scr[:, base - halo:base, :] = cbuf_ref[...]

    xp_scr[:, base:base + rows, :] = qkv_ref[...]

    row = lax.broadcasted_iota(jnp.int32, (rows, rows), 0)
    col = lax.broadcasted_iota(jnp.int32, (rows, rows), 1)
    incl = row >= col
    strict = row > col
    eye = (row == col).astype(F32)
    n_double = max(0, int(math.ceil(math.log2(rows))) - 1)

    ys, beta_all, gcum, gcum_t, exp_g, exp_rem, exp_last = ([] for _ in range(7))
    for b in range(nb):
        acc = cw_ref[0:1, :] * xp_scr[b, base - halo:base - halo + rows, :]
        for j in range(1, n_taps):
            acc = acc + cw_ref[j:j + 1, :] * xp_scr[b, base - halo + j:base - halo + j + rows, :]
        ys.append(_silu(acc))
        tail = xp_scr[b, base + n_valid - halo:base + n_valid, :]
        xp_scr[b, base - halo:base, :] = tail
        nbuf_ref[b] = tail

        ab = ab_ref[b]
        sp = jnp.maximum(ab + dtb_ref[...], 0.0) + jnp.log1p(jnp.exp(-jnp.abs(ab + dtb_ref[...])))
        g_all = -jnp.exp(alog_ref[...]) * sp
        bt = jax.nn.sigmoid(ab)
        if n_valid < rows:
            valid = lax.broadcasted_iota(jnp.int32, (rows, LANES), 0) < n_valid
            g_all = jnp.where(valid, g_all, 0.0)
            bt = jnp.where(valid, bt, 0.0)
        beta_all.append(bt)
        gc = _dot(incl.astype(F32), g_all, HIGHEST)
        gcum.append(gc)
        gcum_t.append(lax.dot_general(g_all, (col >= row).astype(F32), TN_DIMS, precision=HIGHEST,
                                      preferred_element_type=F32))
        glast = gc[rows - 1:rows, :]
        exp_g.append(jnp.exp(gc))
        exp_rem.append(jnp.exp(glast - gc))
        exp_last.append(jnp.exp(glast))

    units = [(b, h) for b in range(nb) for h in range(heads)]
    qn, kn, vb, gamma, beta, eg, er, el, kb, qb = ({} for _ in range(10))
    for b, h in units:
        y, u_ = ys[b], (b, h)
        qh = y[:, h * dk:(h + 1) * dk]
        kh = y[:, qk_w + h * dk:qk_w + (h + 1) * dk]
        qn[u_] = qh * lax.rsqrt(jnp.sum(qh * qh, axis=-1, keepdims=True) + EPS) * dk ** -0.5
        kn[u_] = kh * lax.rsqrt(jnp.sum(kh * kh, axis=-1, keepdims=True) + EPS)
        beta[u_] = beta_all[b][:, heads + h:heads + h + 1]
        vb[u_] = y[:, 2 * qk_w + h * dv:2 * qk_w + (h + 1) * dv] * beta[u_]
        eg[u_] = exp_g[b][:, h:h + 1]
        er[u_] = exp_rem[b][:, h:h + 1]
        el[u_] = exp_last[b][:, h:h + 1]
        gamma[u_] = jnp.exp(jnp.where(incl, gcum[b][:, h:h + 1] - gcum_t[b][h:h + 1, :], NEG_INF))
        kb[u_] = kn[u_].astype(BF16)
        qb[u_] = qn[u_].astype(BF16)

    kk = {u_: _dot_nt(kb[u_], kb[u_]) for u_ in units}
    qk = {u_: _dot_nt(qb[u_], kb[u_]) for u_ in units}
    pw = {u_: -jnp.where(strict, kk[u_] * gamma[u_] * beta[u_], 0.0) for u_ in units}
    tinv = {u_: eye + pw[u_] for u_ in units}
    for _ in range(n_double):
        pwb = {u_: pw[u_].astype(BF16) for u_ in units}
        pw = {u_: _dot(pwb[u_], pwb[u_]) for u_ in units}
        tinv = {u_: tinv[u_] + _dot(tinv[u_].astype(BF16), pw[u_].astype(BF16)) for u_ in units}
    sol = {u_: _dot(tinv[u_].astype(BF16),
                    jnp.concatenate([vb[u_], kn[u_] * (beta[u_] * eg[u_])], axis=-1).astype(BF16))
           for u_ in units}

    s_old = {(b, h): s_scr[b, h] for b, h in units}
    sb = {u_: s_old[u_].astype(BF16) for u_ in units}
    ws = {u_: _dot(sol[u_][:, dv:].astype(BF16), sb[u_]) for u_ in units}
    qs = {u_: _dot((qn[u_] * eg[u_]).astype(BF16), sb[u_]) for u_ in units}
    eb = {u_: (sol[u_][:, :dv] - ws[u_]).astype(BF16) for u_ in units}
    o = {u_: qs[u_] + _dot((qk[u_] * gamma[u_]).astype(BF16), eb[u_]) for u_ in units}
    for b, h in units:
        u_ = (b, h)
        s_scr[b, h] = s_old[u_] * el[u_] + lax.dot_general((kn[u_] * er[u_]).astype(BF16), eb[u_], TN_DIMS,
                                                           preferred_element_type=F32)
        zh = z_ref[b, :, h * dv:(h + 1) * dv]
        o_ref[b, :, h * dv:(h + 1) * dv] = (_rms(o[u_], gon_ref[...]) * _silu(zh)).astype(o_ref.dtype)

    @pl.when(li == pl.num_programs(1) - 1)
    def _():
        sout_ref[...] = s_scr[...]


def delta_mixer(qkv, ab, z, conv_buf, s0, conv_w, a_log, dt_bias, g_onorm, *, rows, n_valid, nb):
    bsz, lp, ch = qkv.shape
    _, heads, dk, dv = s0.shape
    n_taps = conv_w.shape[0]
    n_blk = lp // rows
    assert lp % rows == 0 and (n_valid == rows or n_blk == 1) and bsz % nb == 0 and n_valid >= n_taps - 1
    pad_h = lambda v: jnp.pad(v.astype(F32), (0, LANES - heads)).reshape(1, LANES)
    kern = functools.partial(_delta_kernel, rows=rows, n_valid=n_valid, nb=nb, heads=heads, dk=dk, dv=dv,
                             n_taps=n_taps)
    return pl.pallas_call(
        kern,
        grid=(bsz // nb, n_blk),
        in_specs=[pl.BlockSpec((nb, rows, ch), lambda b, l: (b, l, 0)),
                  pl.BlockSpec((nb, rows, LANES), lambda b, l: (b, l, 0)),
                  pl.BlockSpec((nb, rows, heads * dv), lambda b, l: (b, l, 0)),
                  pl.BlockSpec((nb, n_taps - 1, ch), lambda b, l: (b, 0, 0)),
                  pl.BlockSpec((nb, heads, dk, dv), lambda b, l: (b, 0, 0, 0)),
                  pl.BlockSpec((n_taps, ch), lambda b, l: (0, 0)),
                  pl.BlockSpec((1, LANES), lambda b, l: (0, 0)),
                  pl.BlockSpec((1, LANES), lambda b, l: (0, 0)),
                  pl.BlockSpec((1, dv), lambda b, l: (0, 0))],
        out_specs=[pl.BlockSpec((nb, rows, heads * dv), lambda b, l: (b, l, 0)),
                   pl.BlockSpec((nb, n_taps - 1, ch), lambda b, l: (b, 0, 0)),
                   pl.BlockSpec((nb, heads, dk, dv), lambda b, l: (b, 0, 0, 0))],
        out_shape=[jax.ShapeDtypeStruct((bsz, lp, heads * dv), BF16),
                   jax.ShapeDtypeStruct((bsz, n_taps - 1, ch), F32),
                   jax.ShapeDtypeStruct((bsz, heads, dk, dv), F32)],
        scratch_shapes=[pltpu.VMEM((nb, SUBLANES + rows, ch), F32),
                        pltpu.VMEM((nb, heads, dk, dv), F32)],
        compiler_params=_cparams("parallel", "arbitrary"),
        name="delta_mixer",
    )(qkv, ab, z, conv_buf, s0, conv_w, pad_h(a_log), pad_h(dt_bias), g_onorm.reshape(1, dv))


def _mem_attn_kernel(q_ref, k_ref, v_ref, o_ref, *, heads, hd):
    q = q_ref[0]
    k = k_ref[0].astype(BF16)
    v = v_ref[0].astype(BF16)
    lane_head = lax.broadcasted_iota(jnp.int32, q.shape, 1) // hd
    out = jnp.zeros(q.shape, F32)
    for h in range(heads):
        sel = lane_head == h
        s = _dot_nt(jnp.where(sel, q, 0.0).astype(BF16), k) * hd ** -0.5
        p = jnp.exp(s - jnp.max(s, axis=-1, keepdims=True))
        p = p / jnp.sum(p, axis=-1, keepdims=True)
        out = out + jnp.where(sel, _dot(p.astype(BF16), v), 0.0)
    o_ref[0] = out


def mem_attention(q, mk, mv, heads):
    bsz, lp, w = q.shape
    m = mk.shape[1]
    rows = min(lp, 512)
    assert lp % rows == 0
    return pl.pallas_call(
        functools.partial(_mem_attn_kernel, heads=heads, hd=w // heads),
        grid=(bsz, lp // rows),
        in_specs=[pl.BlockSpec((1, rows, w), lambda b, l: (b, l, 0)),
                  pl.BlockSpec((1, m, w), lambda b, l: (b, 0, 0)),
                  pl.BlockSpec((1, m, w), lambda b, l: (b, 0, 0))],
        out_specs=pl.BlockSpec((1, rows, w), lambda b, l: (b, l, 0)),
        out_shape=jax.ShapeDtypeStruct((bsz, lp, w), F32),
        compiler_params=_cparams("parallel", "parallel"),
        name="mem_attention",
    )(q, mk, mv)


def _router_kernel(x_ref, g_ref, w_ref, b_ref, t_ref, route_ref, gates_ref, counts_ref, cnt_scr,
                   *, n_groups, per_group):
    @pl.when(pl.program_id(0) == 0)
    def _():
        cnt_scr[...] = jnp.zeros_like(cnt_scr)

    t = _rms(x_ref[...], g_ref[...])
    t_ref[...] = t.astype(BF16)
    logits = _dot(t, w_ref[...], HIGHEST) + b_ref[...]
    lane = lax.broadcasted_iota(jnp.int32, logits.shape, 1)

    def first_argmax(v):
        m = jnp.max(v, axis=-1, keepdims=True)
        return m, jnp.min(jnp.where(v == m, lane, LANES), axis=-1, keepdims=True)

    gl = jnp.where(lane < n_groups, logits, NEG_INF)
    gmax, gidx = first_argmax(gl)
    g_val = 1.0 / jnp.sum(jnp.exp(gl - gmax), axis=-1, keepdims=True)
    lo = n_groups + gidx * per_group
    el = jnp.where(lane >= lo, jnp.where(lane < lo + per_group, logits, NEG_INF), NEG_INF)
    m1, i1 = first_argmax(el)
    m2, i2 = first_argmax(jnp.where(lane == i1, NEG_INF, el))
    ex = jnp.exp(m2 - m1)
    den = 1.0 + ex
    gate1 = g_val * (1.0 / den)
    gate2 = g_val * (ex / den)

    tm = logits.shape[0]
    picked = jnp.where(lane == i1, 1.0, jnp.where(lane == i2, 1.0, 0.0))
    earlier = (lax.broadcasted_iota(jnp.int32, (tm, tm), 0)
               > lax.broadcasted_iota(jnp.int32, (tm, tm), 1)).astype(BF16)
    before = cnt_scr[...] + _dot(earlier, picked.astype(BF16))
    rank1 = jnp.sum(jnp.where(lane == i1, before, 0.0), axis=-1, keepdims=True).astype(jnp.int32)
    rank2 = jnp.sum(jnp.where(lane == i2, before, 0.0), axis=-1, keepdims=True).astype(jnp.int32)
    cnt_scr[...] = cnt_scr[...] + jnp.sum(picked, axis=0, keepdims=True)
    counts_ref[...] = cnt_scr[...]

    route_ref[...] = jnp.where(lane == 0, i1 - n_groups, jnp.where(lane == 1, i2 - n_groups,
                               jnp.where(lane == 2, rank1, jnp.where(lane == 3, rank2, 0))))
    gates_ref[...] = jnp.where(lane == 0, gate1, jnp.where(lane == 1, gate2, 0.0))


def moe_router(x, g, w_rg, b_rg, w_re, b_re):
    t, d = x.shape
    n_groups, n_exp = w_rg.shape[1], w_re.shape[1]
    assert n_groups + n_exp <= LANES and TOP_K * t < 2 ** 24
    pad = LANES - n_groups - n_exp
    w = jnp.pad(jnp.concatenate([w_rg, w_re], axis=1), ((0, 0), (0, pad)))
    b = jnp.pad(jnp.concatenate([b_rg, b_re]), (0, pad)).reshape(1, LANES)
    tm = _token_tile(t)
    kern = functools.partial(_router_kernel, n_groups=n_groups, per_group=n_exp // n_groups)
    tn, route, gates, counts = pl.pallas_call(
        kern,
        grid=(t // tm,),
        in_specs=[pl.BlockSpec((tm, d), lambda i: (i, 0)),
                  pl.BlockSpec((1, d), lambda i: (0, 0)),
                  pl.BlockSpec((d, LANES), lambda i: (0, 0)),
                  pl.BlockSpec((1, LANES), lambda i: (0, 0))],
        out_specs=[pl.BlockSpec((tm, d), lambda i: (i, 0)),
                   pl.BlockSpec((tm, LANES), lambda i: (i, 0)),
                   pl.BlockSpec((tm, LANES), lambda i: (i, 0)),
                   pl.BlockSpec((1, LANES), lambda i: (0, 0))],
        out_shape=[jax.ShapeDtypeStruct((t, d), BF16),
                   jax.ShapeDtypeStruct((t, LANES), jnp.int32),
                   jax.ShapeDtypeStruct((t, LANES), F32),
                   jax.ShapeDtypeStruct((1, LANES), F32)],
        scratch_shapes=[pltpu.VMEM((1, LANES), F32)],
        compiler_params=_cparams("arbitrary"),
        name="moe_router",
    )(x, g.reshape(1, d), w, b)
    return tn, route, gates, counts[0, n_groups:n_groups + n_exp].astype(jnp.int32)


def _expert_kernel(te_ref, nu_ref, xs_ref, wg_ref, wu_ref, wd_ref, ys_ref):
    i = pl.program_id(0)

    @pl.when(i < nu_ref[0])
    def _():
        x = xs_ref[...]
        h = _silu(_dot(x, wg_ref[0].astype(BF16))) * _dot(x, wu_ref[0].astype(BF16))
        ys_ref[...] = _dot(h.astype(BF16), wd_ref[0].astype(BF16))

    @pl.when(i >= nu_ref[0])
    def _():
        ys_ref[...] = jnp.zeros_like(ys_ref)


def expert_ffn(tile_expert, n_used, xs, w_gate, w_up, w_down, tile):
    n_slots, d = xs.shape
    ff = w_gate.shape[-1]
    grid_spec = pltpu.PrefetchScalarGridSpec(
        num_scalar_prefetch=2,
        grid=(n_slots // tile,),
        in_specs=[pl.BlockSpec((tile, d), lambda i, te, nu: (i, 0)),
                  pl.BlockSpec((1, d, ff), lambda i, te, nu: (te[i], 0, 0)),
                  pl.BlockSpec((1, d, ff), lambda i, te, nu: (te[i], 0, 0)),
                  pl.BlockSpec((1, ff, d), lambda i, te, nu: (te[i], 0, 0))],
        out_specs=pl.BlockSpec((tile, d), lambda i, te, nu: (i, 0)),
    )
    return pl.pallas_call(
        _expert_kernel,
        grid_spec=grid_spec,
        out_shape=jax.ShapeDtypeStruct((n_slots, d), F32),
        compiler_params=_cparams("arbitrary"),
        name="expert_ffn",
    )(tile_expert, n_used, xs, w_gate, w_up, w_down)


def _combine_kernel(x_ref, ya_ref, yb_ref, gates_ref, g_ref, o_ref, *, final_norm):
    gates = gates_ref[...]
    out = x_ref[...] + (ya_ref[...] * gates[:, 0:1] + yb_ref[...] * gates[:, 1:2])
    if final_norm:
        out = _rms(out, g_ref[...])
    o_ref[...] = out


def moe_combine(x, ya, yb, gates, g_final, final_norm):
    t, d = x.shape
    tm = _token_tile(t)
    row = pl.BlockSpec((tm, d), lambda i: (i, 0))
    return pl.pallas_call(
        functools.partial(_combine_kernel, final_norm=final_norm),
        grid=(t // tm,),
        in_specs=[row, row, row, pl.BlockSpec((tm, LANES), lambda i: (i, 0)),
                  pl.BlockSpec((1, d), lambda i: (0, 0))],
        out_specs=row,
        out_shape=jax.ShapeDtypeStruct((t, d), F32),
        compiler_params=_cparams("parallel"),
        name="moe_combine",
    )(x, ya, yb, gates, g_final.reshape(1, d))


def hier_moe(x, g_ffn, w_rg, b_rg, w_re, b_re, w_gate, w_up, w_down, g_final, final_norm):
    t, d = x.shape
    n_exp = w_gate.shape[0]
    tile = 256 if t >= 8192 else 64
    tn, route, gates, counts = moe_router(x, g_ffn, w_rg, b_rg, w_re, b_re)
    pick_e = route[:, :TOP_K]
    pick_rank = route[:, TOP_K:2 * TOP_K]
    n_pick = t * TOP_K
    experts = jnp.arange(n_exp, dtype=jnp.int32)
    starts = jnp.cumsum(counts) - counts
    padded = (counts + tile - 1) // tile * tile
    pend = jnp.cumsum(padded)
    pstart = pend - padded
    n_tiles = (n_pick + n_exp * (tile - 1)) // tile
    n_slots = n_tiles * tile
    tile_start = jnp.arange(n_tiles, dtype=jnp.int32) * tile
    tile_expert = jnp.minimum(jnp.sum(pend[None, :] <= tile_start[:, None], axis=1), n_exp - 1).astype(jnp.int32)
    n_used = (pend[-1:] // tile).astype(jnp.int32)
    slot_of_pick = jnp.sum(jnp.where(pick_e[..., None] == experts, pstart, 0), axis=-1) + pick_rank

    order = jnp.argsort(pick_e.reshape(-1))
    per_slot = lambda v: jnp.repeat(v[tile_expert], tile)
    rank = jnp.arange(n_slots, dtype=jnp.int32) - per_slot(pstart)
    sorted_pos = jnp.minimum(per_slot(starts) + rank, n_pick - 1)
    src = jnp.where(rank < per_slot(counts), order.at[sorted_pos].get(mode="promise_in_bounds") // TOP_K, 0)

    xs = tn.at[src].get(mode="promise_in_bounds")
    ys = expert_ffn(tile_expert, n_used, xs, w_gate, w_up, w_down, tile)
    ya = ys.at[slot_of_pick[:, 0]].get(mode="promise_in_bounds")
    yb = ys.at[slot_of_pick[:, 1]].get(mode="promise_in_bounds")
    return moe_combine(x, ya, yb, gates, g_final, final_norm)


def _latent_q_kernel(x_ref, gkv_ref, wdkv_ref, glat_ref, cos_ref, sin_ref, gmix_ref, winb_ref, gq_ref,
                     wuq_ref, wuk_ref, c_ref, kr_ref, cb_ref, krb_ref, qm_ref, ql_ref, qr_ref,
                     *, heads, kv, rd, nope, q_lora, scale):
    x = x_ref[...]
    xr = x * lax.rsqrt(jnp.mean(x * x, axis=-1, keepdims=True) + EPS)
    cos = cos_ref[...]
    sin = sin_ref[...]

    ckr = _dot((xr * gkv_ref[...]).astype(BF16), wdkv_ref[...])
    c = _rms(ckr[:, :kv], glat_ref[...])
    kr = ckr[:, kv:kv + rd] * cos + ckr[:, kv + LANES:kv + LANES + rd] * sin
    c_ref[...] = c
    kr_ref[...] = kr
    cb_ref[...] = c.astype(BF16)
    krb_ref[...] = kr.astype(BF16)

    proj = _dot((xr * gmix_ref[...]).astype(BF16), winb_ref[...])
    qm_ref[...] = proj[:, q_lora:]
    cq = _rms(proj[:, :q_lora], gq_ref[...]).astype(BF16)
    q = _dot(cq, wuq_ref[...])
    hw = nope + 2 * LANES
    for h in range(heads):
        qn = q[:, h * hw:h * hw + nope].astype(BF16)
        qrope = (q[:, h * hw + nope:h * hw + nope + rd] * cos
                 + q[:, h * hw + nope + LANES:h * hw + nope + LANES + rd] * sin)
        ql_ref[h] = (_dot(qn, wuk_ref[h]) * scale).astype(BF16)
        qr_ref[h] = (qrope * scale).astype(BF16)


def _swap_halves(w):
    half = w.shape[-1] // 2
    return jnp.concatenate([w[..., half:], w[..., :half]], axis=-1)


def latent_and_queries(x, g_kv_in, w_dkv, g_lat, cos_full, sin_signed, g_mix, w_in_b, g_q, w_uq, w_uk,
                       scale, table_blocks):
    t, d = x.shape
    kv = g_lat.shape[0]
    rd = w_dkv.shape[1] - kv
    _, heads, nope = w_uk.shape
    q_lora = g_q.shape[0]
    mem_w = w_in_b.shape[1] - q_lora
    assert rd <= LANES and kv % LANES == 0 and nope % LANES == 0 and q_lora % LANES == 0
    padl = lambda w: jnp.pad(w, ((0, 0), (0, LANES - rd)))
    w_r = w_dkv[:, kv:]
    wdkv_ext = jnp.concatenate([w_dkv[:, :kv], padl(w_r), padl(_swap_halves(w_r))], axis=1).astype(BF16)
    w_uq_h = w_uq.reshape(q_lora, heads, nope + rd)
    w_qr = w_uq_h[..., nope:]
    pad3 = lambda w: jnp.pad(w, ((0, 0), (0, 0), (0, LANES - rd)))
    wuq_ext = jnp.concatenate([w_uq_h[..., :nope], pad3(w_qr), pad3(_swap_halves(w_qr))], axis=-1)
    hw = nope + 2 * LANES
    wuq_ext = wuq_ext.reshape(q_lora, heads * hw).astype(BF16)
    wuk_t = jnp.transpose(w_uk, (1, 2, 0)).astype(BF16)
    tm = _token_tile(t, 256)
    nblk = table_blocks(tm)
    kern = functools.partial(_latent_q_kernel, heads=heads, kv=kv, rd=rd, nope=nope, q_lora=q_lora,
                             scale=scale)
    full = lambda shape: pl.BlockSpec(shape, lambda i: (0,) * len(shape))
    return pl.pallas_call(
        kern,
        grid=(t // tm,),
        in_specs=[pl.BlockSpec((tm, d), lambda i: (i, 0)),
                  full((1, d)), full(wdkv_ext.shape), full((1, kv)),
                  pl.BlockSpec((tm, rd), lambda i: (i % nblk, 0)),
                  pl.BlockSpec((tm, rd), lambda i: (i % nblk, 0)),
                  full((1, d)), full(w_in_b.shape), full((1, q_lora)),
                  full(wuq_ext.shape), full(wuk_t.shape)],
        out_specs=[pl.BlockSpec((tm, kv), lambda i: (i, 0)),
                   pl.BlockSpec((tm, rd), lambda i: (i, 0)),
                   pl.BlockSpec((tm, kv), lambda i: (i, 0)),
                   pl.BlockSpec((tm, rd), lambda i: (i, 0)),
                   pl.BlockSpec((tm, mem_w), lambda i: (i, 0)),
                   pl.BlockSpec((heads, tm, kv), lambda i: (0, i, 0)),
                   pl.BlockSpec((heads, tm, rd), lambda i: (0, i, 0))],
        out_shape=[jax.ShapeDtypeStruct((t, kv), F32),
                   jax.ShapeDtypeStruct((t, rd), F32),
                   jax.ShapeDtypeStruct((t, kv), BF16),
                   jax.ShapeDtypeStruct((t, rd), BF16),
                   jax.ShapeDtypeStruct((t, mem_w), F32),
                   jax.ShapeDtypeStruct((heads, t, kv), BF16),
                   jax.ShapeDtypeStruct((heads, t, rd), BF16)],
        compiler_params=_cparams("parallel"),
        name="latent_and_queries",
    )(x, g_kv_in.reshape(1, d), wdkv_ext, g_lat.reshape(1, kv), cos_full, sin_signed,
      g_mix.reshape(1, d), w_in_b.astype(BF16), g_q.reshape(1, q_lora), wuq_ext, wuk_t)


def _mla_prompt_kernel(qi_ref, kj_ref, ql_ref, qr_ref, c_ref, kr_ref, wuv_ref, o_ref,
                       m_scr, l_scr, acc_scr, *, heads, tq, vh, sub):
    p_idx = pl.program_id(1)
    qi = qi_ref[p_idx]
    kj = kj_ref[p_idx]
    kv = c_ref.shape[-1]

    @pl.when(kj == 0)
    def _():
        m_scr[...] = jnp.full_like(m_scr, NEG_INF)
        l_scr[...] = jnp.zeros_like(l_scr)
        acc_scr[...] = jnp.zeros_like(acc_scr)

    def step(masked):
        c = c_ref[...]
        kr = kr_ref[...]
        for h in range(heads):
            for r0 in range(0, tq, sub):
                rows = slice(h * tq + r0, h * tq + r0 + sub)
                s = _dot_nt(ql_ref[h, r0:r0 + sub, :], c) + _dot_nt(qr_ref[h, r0:r0 + sub, :], kr)
                if masked:
                    qpos = r0 + lax.broadcasted_iota(jnp.int32, (sub, tq), 0)
                    kpos = lax.broadcasted_iota(jnp.int32, (sub, tq), 1)
                    s = jnp.where(kpos <= qpos, s, NEG_INF)
                m_prev = m_scr[rows]
                m_new = jnp.maximum(m_prev, jnp.max(s, axis=-1, keepdims=True))
                alpha = jnp.exp(m_prev - m_new)
                p = jnp.exp(s - _lane_tile(m_new, tq // LANES))
                l_scr[rows] = alpha * l_scr[rows] + jnp.sum(p, axis=-1, keepdims=True)
                acc_scr[rows] = _lane_tile(alpha, kv // LANES) * acc_scr[rows] + _dot(p.astype(BF16), c)
                m_scr[rows] = m_new

    @pl.when(kj < qi)
    def _():
        step(False)

    @pl.when(kj == qi)
    def _():
        step(True)
        for h in range(heads):
            rows = slice(h * tq, (h + 1) * tq)
            out = acc_scr[rows] / _lane_tile(l_scr[rows], kv // LANES)
            o_ref[:, h * vh:(h + 1) * vh] = _dot(out.astype(BF16), wuv_ref[h])


def mla_prompt_attention(ql, qr, cb, krb, w_uv, bsz, seq, tq=512, sub=512):
    heads, t, kv = ql.shape
    rd = qr.shape[-1]
    vh = w_uv.shape[-1]
    tq = min(tq, seq)
    assert seq % tq == 0
    nq = seq // tq
    pairs = [(i, j) for i in range(nq) for j in range(i + 1)]
    qi = jnp.asarray([p[0] for p in pairs], jnp.int32)
    kj = jnp.asarray([p[1] for p in pairs], jnp.int32)
    wuv_h = jnp.transpose(w_uv, (1, 0, 2)).astype(BF16)
    grid_spec = pltpu.PrefetchScalarGridSpec(
        num_scalar_prefetch=2,
        grid=(bsz, len(pairs)),
        in_specs=[pl.BlockSpec((heads, tq, kv), lambda b, p, qi, kj: (0, b * nq + qi[p], 0)),
                  pl.BlockSpec((heads, tq, rd), lambda b, p, qi, kj: (0, b * nq + qi[p], 0)),
                  pl.BlockSpec((tq, kv), lambda b, p, qi, kj: (b * nq + kj[p], 0)),
                  pl.BlockSpec((tq, rd), lambda b, p, qi, kj: (b * nq + kj[p], 0)),
                  pl.BlockSpec((heads, kv, vh), lambda b, p, qi, kj: (0, 0, 0))],
        out_specs=pl.BlockSpec((tq, heads * vh), lambda b, p, qi, kj: (b * nq + qi[p], 0)),
        scratch_shapes=[pltpu.VMEM((heads * tq, LANES), F32),
                        pltpu.VMEM((heads * tq, LANES), F32),
                        pltpu.VMEM((heads * tq, kv), F32)],
    )
    sub = min(sub, tq)
    assert tq % sub == 0 and tq % LANES == 0 and kv % LANES == 0
    return pl.pallas_call(
        functools.partial(_mla_prompt_kernel, heads=heads, tq=tq, vh=vh, sub=sub),
        grid_spec=grid_spec,
        out_shape=jax.ShapeDtypeStruct((t, heads * vh), F32),
        compiler_params=_cparams("parallel", "arbitrary"),
        name="mla_prompt_attention",
    )(qi, kj, ql, qr, cb, krb, wuv_h)


def _mla_paged_kernel(pt_ref, ql_ref, qr_ref, cn_ref, krn_ref, clat_hbm, ckr_hbm, wuv_ref, o_ref,
                      kbuf, rbuf, sems, m_scr, l_scr, acc_scr,
                      *, heads, n_new, page, pages_per_chunk, chunks_per_seq, vh):
    g = pl.program_id(0)
    total = pl.num_programs(0)
    ci = g % chunks_per_seq
    slot = g % 2

    def page_copies(chunk, slot_, p):
        pg = pt_ref[chunk * pages_per_chunk + p]
        rows = pl.ds(p * page, page)
        return (pltpu.make_async_copy(clat_hbm.at[pg], kbuf.at[slot_, rows], sems.at[0, slot_]),
                pltpu.make_async_copy(ckr_hbm.at[pg], rbuf.at[slot_, :, rows], sems.at[1, slot_]))

    def start_chunk(chunk, slot_):
        for p in range(pages_per_chunk):
            for cp in page_copies(chunk, slot_, p):
                cp.start()

    @pl.when(g == 0)
    def _():
        start_chunk(0, 0)

    @pl.when(g + 1 < total)
    def _():
        start_chunk(g + 1, 1 - slot)

    for p in range(pages_per_chunk):
        for cp in page_copies(g, slot, p):
            cp.wait()

    @pl.when(ci == 0)
    def _():
        m_scr[...] = jnp.full_like(m_scr, NEG_INF)
        l_scr[...] = jnp.zeros_like(l_scr)
        acc_scr[...] = jnp.zeros_like(acc_scr)

    ql = ql_ref[0]
    qr = qr_ref[0]

    def update(s, v):
        m_prev = m_scr[...]
        m_new = jnp.maximum(m_prev, jnp.max(s, axis=-1, keepdims=True))
        alpha = jnp.exp(m_prev - m_new)
        p = jnp.exp(s - m_new)
        l_scr[...] = alpha * l_scr[...] + jnp.sum(p, axis=-1, keepdims=True)
        acc_scr[...] = alpha * acc_scr[...] + _dot(p.astype(BF16), v)
        m_scr[...] = m_new

    k = kbuf[slot].astype(BF16)
    update(_dot_nt(ql, k) + _dot(qr, rbuf[slot].astype(BF16)), k)

    @pl.when(ci == chunks_per_seq - 1)
    def _():
        cn = cn_ref[0].astype(BF16)
        s = _dot_nt(ql, cn) + _dot_nt(qr, krn_ref[0].astype(BF16))
        qpos = lax.broadcasted_iota(jnp.int32, s.shape, 0) % n_new
        kpos = lax.broadcasted_iota(jnp.int32, s.shape, 1)
        update(jnp.where(kpos <= qpos, s, NEG_INF), cn)
        out = (acc_scr[...] / l_scr[...]).astype(BF16)
        for h in range(heads):
            o_ref[0, :, h * vh:(h + 1) * vh] = _dot(out, wuv_ref[h])


def mla_paged_attention(page_table, ql, qr, c_new, kr_new, cache_lat, cache_kr, w_uv):
    bsz, n_pages = page_table.shape
    _, page, kv = cache_lat.shape
    rd = cache_kr.shape[-1]
    q_rows = ql.shape[1]
    n_new = c_new.shape[1]
    heads, vh = w_uv.shape[1], w_uv.shape[2]
    pages_per_chunk = min(n_pages, 32)
    assert n_pages % pages_per_chunk == 0
    chunks_per_seq = n_pages // pages_per_chunk
    chunk_rows = pages_per_chunk * page
    new_pad = -n_new % SUBLANES
    c_new = jnp.pad(c_new, ((0, 0), (0, new_pad), (0, 0)))
    kr_new = jnp.pad(kr_new, ((0, 0), (0, new_pad), (0, 0)))
    wuv_h = jnp.transpose(w_uv, (1, 0, 2)).astype(BF16)
    cache_kr_t = jnp.transpose(cache_kr, (0, 2, 1))
    kern = functools.partial(_mla_paged_kernel, heads=heads, n_new=n_new, page=page,
                             pages_per_chunk=pages_per_chunk, chunks_per_seq=chunks_per_seq, vh=vh)
    seq_of = lambda g, pt: (g // chunks_per_seq, 0, 0)
    grid_spec = pltpu.PrefetchScalarGridSpec(
        num_scalar_prefetch=1,
        grid=(bsz * chunks_per_seq,),
        in_specs=[pl.BlockSpec((1, q_rows, kv), seq_of),
                  pl.BlockSpec((1, q_rows, rd), seq_of),
                  pl.BlockSpec((1, n_new + new_pad, kv), seq_of),
                  pl.BlockSpec((1, n_new + new_pad, rd), seq_of),
                  pl.BlockSpec(memory_space=pl.ANY),
                  pl.BlockSpec(memory_space=pl.ANY),
                  pl.BlockSpec((heads, kv, vh), lambda g, pt: (0, 0, 0))],
        out_specs=pl.BlockSpec((1, q_rows, heads * vh), seq_of),
        scratch_shapes=[pltpu.VMEM((2, chunk_rows, kv), F32),
                        pltpu.VMEM((2, rd, chunk_rows), F32),
                        pltpu.SemaphoreType.DMA((2, 2)),
                        pltpu.VMEM((q_rows, 1), F32),
                        pltpu.VMEM((q_rows, 1), F32),
                        pltpu.VMEM((q_rows, kv), F32)],
    )
    return pl.pallas_call(
        kern,
        grid_spec=grid_spec,
        out_shape=jax.ShapeDtypeStruct((bsz, q_rows, heads * vh), F32),
        compiler_params=_cparams("arbitrary"),
        name="mla_paged_attention",
    )(page_table.reshape(-1), ql, qr, c_new, kr_new, cache_lat, cache_kr_t, wuv_h)


def _rope_tables(pos, rd):
    inv = ROPE_THETA ** (-jnp.arange(0, rd, 2, dtype=F32) / rd)
    ang = pos.astype(F32)[:, None] * inv[None, :]
    cos, sin = jnp.cos(ang), jnp.sin(ang)
    return jnp.concatenate([cos, cos], axis=-1), jnp.concatenate([-sin, sin], axis=-1)


def _run_group(x3, pos, mem_k, mem_v, conv_buf, d_state, paged, p):
    bsz, seq, d = x3.shape
    t = bsz * seq
    x = x3.reshape(t, d)
    heads_dn, dk, dv = d_state.shape[1:]
    conv_ch = p["conv_w"].shape[-1]
    v_w = heads_dn * dv
    mem_heads = mem_k.shape[3]
    mem_w = mem_heads * mem_k.shape[4]
    mem_tokens = mem_k.shape[2]

    w_in = p["w_in_a"][0]
    w_ab = jnp.pad(w_in[:, conv_ch + v_w:conv_ch + v_w + 2 * heads_dn], ((0, 0), (0, LANES - 2 * heads_dn)))
    w_in_r = jnp.concatenate([w_in[:, :conv_ch + v_w], w_in[:, conv_ch + v_w + 2 * heads_dn:], w_ab],
                             axis=1).astype(BF16)
    qkv, z, qm, ab = norm_matmul(x, p["g_mix"][0], w_in_r, (conv_ch, v_w, mem_w, LANES), (F32,) * 4)
    rows = DELTA_CHUNK if seq % DELTA_CHUNK == 0 else -(-seq // SUBLANES) * SUBLANES
    seq_p = -(-seq // rows) * rows
    seq3 = lambda a: jnp.pad(a.reshape(bsz, seq, -1), ((0, 0), (0, seq_p - seq), (0, 0)))
    o_dn, new_buf, new_state = delta_mixer(seq3(qkv), seq3(ab), seq3(z), conv_buf, d_state, p["conv_w"][0],
                                           p["a_log"][0], p["dt_bias"][0], p["g_onorm"][0],
                                           rows=rows, n_valid=min(seq, rows),
                                           nb=math.gcd(bsz, 2 if rows >= DELTA_CHUNK else 4))
    o_dn = o_dn[:, :seq].reshape(t, v_w)
    mo = mem_attention(seq3(qm), mem_k[0].reshape(bsz, mem_tokens, mem_w),
                       mem_v[0].reshape(bsz, mem_tokens, mem_w), mem_heads)[:, :seq].reshape(t, mem_w)
    x = proj_residual(o_dn, mo, p["w_out_a"][0].astype(BF16), x)
    x = hier_moe(x, p["g_ffn"][0], p["w_router_group"][0], p["b_router_group"][0], p["w_router_expert"][0],
                 p["b_router_expert"][0], p["w_exp_gate"][0], p["w_exp_up"][0], p["w_exp_down"][0],
                 p["g_final"], False)

    kv = p["g_kv_latent"].shape[0]
    rd = p["w_dkv"].shape[1] - kv
    nope = p["w_uk"].shape[2]
    mla_heads = p["w_uk"].shape[1]
    scale = (nope + rd) ** -0.5
    if paged is None:
        cos_t, sin_t = _rope_tables(pos, rd)
        table_blocks = lambda tm: seq // tm
    else:
        cos_t, sin_t = _rope_tables(jnp.tile(pos, bsz), rd)
        table_blocks = lambda tm: t // tm
    c, kr, cb, krb, qm, ql, qr = latent_and_queries(
        x, p["g_kv_in"], p["w_dkv"], p["g_kv_latent"], cos_t, sin_t, p["g_mix"][1], p["w_in_b"][0],
        p["g_q"][0], p["w_uq"][0], p["w_uk"], scale, table_blocks)
    if paged is None:
        o_mla = mla_prompt_attention(ql, qr, cb, krb, p["w_uv"], bsz, seq)
    else:
        q_rows = -(-mla_heads * seq // 16) * 16
        to_rows = lambda q: jnp.pad(
            jnp.transpose(q.reshape(mla_heads, bsz, seq, -1), (1, 0, 2, 3)).reshape(bsz, mla_heads * seq, -1),
            ((0, 0), (0, q_rows - mla_heads * seq), (0, 0)))
        o_all = mla_paged_attention(paged[2], to_rows(ql), to_rows(qr), c.reshape(bsz, seq, kv),
                                    kr.reshape(bsz, seq, rd), paged[0], paged[1], p["w_uv"])
        vh = p["w_uv"].shape[2]
        o_all = o_all[:, :mla_heads * seq].reshape(bsz, mla_heads, seq, mla_heads, vh)
        o_mla = jnp.stack([o_all[:, h, :, h] for h in range(mla_heads)], axis=2).reshape(t, mla_heads * vh)
    mo = mem_attention(seq3(qm), mem_k[1].reshape(bsz, mem_tokens, mem_w),
                       mem_v[1].reshape(bsz, mem_tokens, mem_w), mem_heads)[:, :seq].reshape(t, mem_w)
    x = proj_residual(o_mla, mo, p["w_out_b"][0].astype(BF16), x)
    y = hier_moe(x, p["g_ffn"][1], p["w_router_group"][1], p["b_router_group"][1], p["w_router_expert"][1],
                 p["b_router_expert"][1], p["w_exp_gate"][1], p["w_exp_up"][1], p["w_exp_down"][1],
                 p["g_final"], True)
    return (y.reshape(bsz, seq, d), c.reshape(bsz, seq, kv), kr.reshape(bsz, seq, rd),
            new_buf[None], new_state[None])


def kernel(x_prompt, x_sample, mem_prompt, state_delta, cache_conv, cache_kv_latent, cache_k_rope, cache_mem_k, cache_mem_v, page_table, g_mix, g_ffn, g_final, w_in_a, conv_w, a_log, dt_bias, g_onorm, w_out_a, g_kv_in, w_dkv, g_kv_latent, w_uk, w_uv, w_in_b, g_q, w_uq, w_out_b, g_mem, w_mem_k, w_mem_v, w_router_group, b_router_group, w_router_expert, b_router_expert, w_exp_gate, w_exp_up, w_exp_down):
    p = dict(g_mix=g_mix, g_ffn=g_ffn, g_final=g_final, w_in_a=w_in_a, conv_w=conv_w, a_log=a_log,
             dt_bias=dt_bias, g_onorm=g_onorm, w_out_a=w_out_a, g_kv_in=g_kv_in, w_dkv=w_dkv,
             g_kv_latent=g_kv_latent, w_uk=w_uk, w_uv=w_uv, w_in_b=w_in_b, g_q=g_q, w_uq=w_uq,
             w_out_b=w_out_b, w_router_group=w_router_group, b_router_group=b_router_group,
             w_router_expert=w_router_expert, b_router_expert=b_router_expert, w_exp_gate=w_exp_gate,
             w_exp_up=w_exp_up, w_exp_down=w_exp_down)
    depth = g_mix.shape[0]
    assert depth == 2 and w_in_a.shape[0] == 1
    bp, lp, d = x_prompt.shape
    bs, ls, _ = x_sample.shape
    mem_tokens = mem_prompt.shape[1]
    mem_heads, mem_hd = cache_mem_k.shape[3], cache_mem_k.shape[4]
    mem_w = mem_heads * mem_hd

    mem_flat = mem_prompt.reshape(bp * mem_tokens, d)
    mkv = [norm_matmul(mem_flat, g_mem[l], jnp.concatenate([w_mem_k[l], w_mem_v[l]], axis=1).astype(BF16),
                       (mem_w, mem_w), (F32, F32)) for l in range(depth)]
    shape5 = (bp, mem_tokens, mem_heads, mem_hd)
    mem_k_prompt = jnp.stack([m[0].reshape(shape5) for m in mkv])
    mem_v_prompt = jnp.stack([m[1].reshape(shape5) for m in mkv])
    conv0 = jnp.zeros((bp,) + cache_conv.shape[2:], F32)
    state0 = jnp.zeros((bp,) + state_delta.shape[2:], F32)
    y_p, c_p, kr_p, conv_p, state_p = _run_group(
        x_prompt, jnp.arange(lp), mem_k_prompt, mem_v_prompt, conv0, state0, None, p)

    past_len = page_table.shape[1] * cache_kv_latent.shape[1]
    y_s, c_s, kr_s, conv_s, state_s = _run_group(
        x_sample, past_len + jnp.arange(ls), cache_mem_k, cache_mem_v, cache_conv[0], state_delta[0],
        (cache_kv_latent, cache_k_rope, page_table), p)

    return (y_p, y_s, c_p, kr_p, c_s, kr_s, state_p, conv_p, state_s, conv_s, mem_k_prompt, mem_v_prompt)
```

```python
import functools
import math

import jax
import jax.numpy as jnp
from jax import lax
from jax.experimental import pallas as pl
from jax.experimental.pallas import tpu as pltpu

F32 = jnp.float32
BF16 = jnp.bfloat16
EPS = 1e-6
ROPE_THETA = 10000.0
TOP_K = 2
DELTA_CHUNK = 64
LANES = 128
SUBLANES = 8
VMEM_LIMIT = 56 * 1024 * 1024
NEG_INF = float("-inf")
HIGHEST = lax.Precision.HIGHEST

NT_DIMS = (((1,), (1,)), ((), ()))
TN_DIMS = (((0,), (0,)), ((), ()))


def _cparams(*sem):
    return pltpu.CompilerParams(dimension_semantics=sem, vmem_limit_bytes=VMEM_LIMIT)


def _rms(x, g):
    return x * lax.rsqrt(jnp.mean(x * x, axis=-1, keepdims=True) + EPS) * g


def _silu(x):
    return x * jax.nn.sigmoid(x)


def _dot(a, b, precision=None):
    return jnp.dot(a, b, preferred_element_type=F32, precision=precision)


def _dot_nt(a, b, precision=None):
    return lax.dot_general(a, b, NT_DIMS, preferred_element_type=F32, precision=precision)


def _lane_tile(x, k):
    return jnp.concatenate([x] * k, axis=1)


def _token_tile(t, cap=512):
    tm = min(t, cap)
    assert t % tm == 0
    return tm


def _norm_matmul_kernel(x_ref, g_ref, w_ref, *out_refs, splits):
    xn = _rms(x_ref[...], g_ref[...]).astype(BF16)
    off = 0
    for o_ref, n in zip(out_refs, splits):
        o_ref[...] = _dot(xn, w_ref[:, off:off + n]).astype(o_ref.dtype)
        off += n


def norm_matmul(x, g, w_bf16, splits, out_dtypes):
    t, d = x.shape
    tm = _token_tile(t)
    n = w_bf16.shape[1]
    assert sum(splits) == n and all(s % LANES == 0 for s in splits)
    return pl.pallas_call(
        functools.partial(_norm_matmul_kernel, splits=tuple(splits)),
        grid=(t // tm,),
        in_specs=[pl.BlockSpec((tm, d), lambda i: (i, 0)),
                  pl.BlockSpec((1, d), lambda i: (0, 0)),
                  pl.BlockSpec((d, n), lambda i: (0, 0))],
        out_specs=[pl.BlockSpec((tm, s), lambda i: (i, 0)) for s in splits],
        out_shape=[jax.ShapeDtypeStruct((t, s), dt) for s, dt in zip(splits, out_dtypes)],
        compiler_params=_cparams("parallel"),
        name="norm_matmul",
    )(x, g.reshape(1, d), w_bf16)


def _proj_residual_kernel(a_ref, b_ref, wa_ref, wb_ref, res_ref, o_ref):
    acc = _dot(a_ref[...].astype(BF16), wa_ref[...]) + _dot(b_ref[...].astype(BF16), wb_ref[...])
    o_ref[...] = res_ref[...] + acc


def proj_residual(a, b, w_bf16, res):
    t, d = res.shape
    ka, kb = a.shape[1], b.shape[1]
    tm = _token_tile(t)
    wa, wb = w_bf16[:ka], w_bf16[ka:]
    return pl.pallas_call(
        _proj_residual_kernel,
        grid=(t // tm,),
        in_specs=[pl.BlockSpec((tm, ka), lambda i: (i, 0)),
                  pl.BlockSpec((tm, kb), lambda i: (i, 0)),
                  pl.BlockSpec((ka, d), lambda i: (0, 0)),
                  pl.BlockSpec((kb, d), lambda i: (0, 0)),
                  pl.BlockSpec((tm, d), lambda i: (i, 0))],
        out_specs=pl.BlockSpec((tm, d), lambda i: (i, 0)),
        out_shape=jax.ShapeDtypeStruct((t, d), F32),
        compiler_params=_cparams("parallel"),
        name="proj_residual",
    )(a, b, wa, wb, res)


def _delta_kernel(qkv_ref, ab_ref, z_ref, cbuf_ref, s0_ref, cw_ref, alog_ref, dtb_ref, gon_ref,
                  o_ref, nbuf_ref, sout_ref, xp_scr, s_scr, *, rows, n_valid, nb, heads, dk, dv, n_taps):
    li = pl.program_id(1)
    halo = n_taps - 1
    base = SUBLANES
    qk_w = heads * dk

    @pl.when(li == 0)
    def _():
        s_scr[...] = s0_ref[...]
        xp_scr[:, base - halo:base, :] = cbuf_ref[...]

    xp_scr[:, base:base + rows, :] = qkv_ref[...]

    row = lax.broadcasted_iota(jnp.int32, (rows, rows), 0)
    col = lax.broadcasted_iota(jnp.int32, (rows, rows), 1)
    incl = row >= col
    strict = row > col
    eye = (row == col).astype(F32)
    n_double = max(0, int(math.ceil(math.log2(rows))) - 1)

    ys, beta_all, gcum, gcum_t, exp_g, exp_rem, exp_last = ([] for _ in range(7))
    for b in range(nb):
        acc = cw_ref[0:1, :] * xp_scr[b, base - halo:base - halo + rows, :]
        for j in range(1, n_taps):
            acc = acc + cw_ref[j:j + 1, :] * xp_scr[b, base - halo + j:base - halo + j + rows, :]
        ys.append(_silu(acc))
        tail = xp_scr[b, base + n_valid - halo:base + n_valid, :]
        xp_scr[b, base - halo:base, :] = tail
        nbuf_ref[b] = tail

        ab = ab_ref[b]
        sp = jnp.maximum(ab + dtb_ref[...], 0.0) + jnp.log1p(jnp.exp(-jnp.abs(ab + dtb_ref[...])))
        g_all = -jnp.exp(alog_ref[...]) * sp
        bt = jax.nn.sigmoid(ab)
        if n_valid < rows:
            valid = lax.broadcasted_iota(jnp.int32, (rows, LANES), 0) < n_valid
            g_all = jnp.where(valid, g_all, 0.0)
            bt = jnp.where(valid, bt, 0.0)
        beta_all.append(bt)
        gc = _dot(incl.astype(F32), g_all, HIGHEST)
        gcum.append(gc)
        gcum_t.append(lax.dot_general(g_all, (col >= row).astype(F32), TN_DIMS, precision=HIGHEST,
                                      preferred_element_type=F32))
        glast = gc[rows - 1:rows, :]
        exp_g.append(jnp.exp(gc))
        exp_rem.append(jnp.exp(glast - gc))
        exp_last.append(jnp.exp(glast))

    units = [(b, h) for b in range(nb) for h in range(heads)]
    qn, kn, vb, gamma, beta, eg, er, el, kb, qb = ({} for _ in range(10))
    for b, h in units:
        y, u_ = ys[b], (b, h)
        qh = y[:, h * dk:(h + 1) * dk]
        kh = y[:, qk_w + h * dk:qk_w + (h + 1) * dk]
        qn[u_] = qh * lax.rsqrt(jnp.sum(qh * qh, axis=-1, keepdims=True) + EPS) * dk ** -0.5
        kn[u_] = kh * lax.rsqrt(jnp.sum(kh * kh, axis=-1, keepdims=True) + EPS)
        beta[u_] = beta_all[b][:, heads + h:heads + h + 1]
        vb[u_] = y[:, 2 * qk_w + h * dv:2 * qk_w + (h + 1) * dv] * beta[u_]
        eg[u_] = exp_g[b][:, h:h + 1]
        er[u_] = exp_rem[b][:, h:h + 1]
        el[u_] = exp_last[b][:, h:h + 1]
        gamma[u_] = jnp.exp(jnp.where(incl, gcum[b][:, h:h + 1] - gcum_t[b][h:h + 1, :], NEG_INF))
        kb[u_] = kn[u_].astype(BF16)
        qb[u_] = qn[u_].astype(BF16)

    kk = {u_: _dot_nt(kb[u_], kb[u_]) for u_ in units}
    qk = {u_: _dot_nt(qb[u_], kb[u_]) for u_ in units}
    pw = {u_: -jnp.where(strict, kk[u_] * gamma[u_] * beta[u_], 0.0) for u_ in units}
    tinv = {u_: eye + pw[u_] for u_ in units}
    for _ in range(n_double):
        pwb = {u_: pw[u_].astype(BF16) for u_ in units}
        pw = {u_: _dot(pwb[u_], pwb[u_]) for u_ in units}
        tinv = {u_: tinv[u_] + _dot(tinv[u_].astype(BF16), pw[u_].astype(BF16)) for u_ in units}
    sol = {u_: _dot(tinv[u_].astype(BF16),
                    jnp.concatenate([vb[u_], kn[u_] * (beta[u_] * eg[u_])], axis=-1).astype(BF16))
           for u_ in units}

    s_old = {(b, h): s_scr[b, h] for b, h in units}
    sb = {u_: s_old[u_].astype(BF16) for u_ in units}
    ws = {u_: _dot(sol[u_][:, dv:].astype(BF16), sb[u_]) for u_ in units}
    qs = {u_: _dot((qn[u_] * eg[u_]).astype(BF16), sb[u_]) for u_ in units}
    eb = {u_: (sol[u_][:, :dv] - ws[u_]).astype(BF16) for u_ in units}
    o = {u_: qs[u_] + _dot((qk[u_] * gamma[u_]).astype(BF16), eb[u_]) for u_ in units}
    for b, h in units:
        u_ = (b, h)
        s_scr[b, h] = s_old[u_] * el[u_] + lax.dot_general((kn[u_] * er[u_]).astype(BF16), eb[u_], TN_DIMS,
                                                           preferred_element_type=F32)
        zh = z_ref[b, :, h * dv:(h + 1) * dv]
        o_ref[b, :, h * dv:(h + 1) * dv] = (_rms(o[u_], gon_ref[...]) * _silu(zh)).astype(o_ref.dtype)

    @pl.when(li == pl.num_programs(1) - 1)
    def _():
        sout_ref[...] = s_scr[...]


def delta_mixer(qkv, ab, z, conv_buf, s0, conv_w, a_log, dt_bias, g_onorm, *, rows, n_valid, nb):
    bsz, lp, ch = qkv.shape
    _, heads, dk, dv = s0.shape
    n_taps = conv_w.shape[0]
    n_blk = lp // rows
    assert lp % rows == 0 and (n_valid == rows or n_blk == 1) and bsz % nb == 0 and n_valid >= n_taps - 1
    pad_h = lambda v: jnp.pad(v.astype(F32), (0, LANES - heads)).reshape(1, LANES)
    kern = functools.partial(_delta_kernel, rows=rows, n_valid=n_valid, nb=nb, heads=heads, dk=dk, dv=dv,
                             n_taps=n_taps)
    return pl.pallas_call(
        kern,
        grid=(bsz // nb, n_blk),
        in_specs=[pl.BlockSpec((nb, rows, ch), lambda b, l: (b, l, 0)),
                  pl.BlockSpec((nb, rows, LANES), lambda b, l: (b, l, 0)),
                  pl.BlockSpec((nb, rows, heads * dv), lambda b, l: (b, l, 0)),
                  pl.BlockSpec((nb, n_taps - 1, ch), lambda b, l: (b, 0, 0)),
                  pl.BlockSpec((nb, heads, dk, dv), lambda b, l: (b, 0, 0, 0)),
                  pl.BlockSpec((n_taps, ch), lambda b, l: (0, 0)),
                  pl.BlockSpec((1, LANES), lambda b, l: (0, 0)),
                  pl.BlockSpec((1, LANES), lambda b, l: (0, 0)),
                  pl.BlockSpec((1, dv), lambda b, l: (0, 0))],
        out_specs=[pl.BlockSpec((nb, rows, heads * dv), lambda b, l: (b, l, 0)),
                   pl.BlockSpec((nb, n_taps - 1, ch), lambda b, l: (b, 0, 0)),
                   pl.BlockSpec((nb, heads, dk, dv), lambda b, l: (b, 0, 0, 0))],
        out_shape=[jax.ShapeDtypeStruct((bsz, lp, heads * dv), BF16),
                   jax.ShapeDtypeStruct((bsz, n_taps - 1, ch), F32),
                   jax.ShapeDtypeStruct((bsz, heads, dk, dv), F32)],
        scratch_shapes=[pltpu.VMEM((nb, SUBLANES + rows, ch), F32),
                        pltpu.VMEM((nb, heads, dk, dv), F32)],
        compiler_params=_cparams("parallel", "arbitrary"),
        name="delta_mixer",
    )(qkv, ab, z, conv_buf, s0, conv_w, pad_h(a_log), pad_h(dt_bias), g_onorm.reshape(1, dv))


def _mem_attn_kernel(q_ref, k_ref, v_ref, o_ref, *, heads, hd):
    q = q_ref[0]
    k = k_ref[0].astype(BF16)
    v = v_ref[0].astype(BF16)
    lane_head = lax.broadcasted_iota(jnp.int32, q.shape, 1) // hd
    out = jnp.zeros(q.shape, F32)
    for h in range(heads):
        sel = lane_head == h
        s = _dot_nt(jnp.where(sel, q, 0.0).astype(BF16), k) * hd ** -0.5
        p = jnp.exp(s - jnp.max(s, axis=-1, keepdims=True))
        p = p / jnp.sum(p, axis=-1, keepdims=True)
        out = out + jnp.where(sel, _dot(p.astype(BF16), v), 0.0)
    o_ref[0] = out


def mem_attention(q, mk, mv, heads):
    bsz, lp, w = q.shape
    m = mk.shape[1]
    rows = min(lp, 512)
    assert lp % rows == 0
    return pl.pallas_call(
        functools.partial(_mem_attn_kernel, heads=heads, hd=w // heads),
        grid=(bsz, lp // rows),
        in_specs=[pl.BlockSpec((1, rows, w), lambda b, l: (b, l, 0)),
                  pl.BlockSpec((1, m, w), lambda b, l: (b, 0, 0)),
                  pl.BlockSpec((1, m, w), lambda b, l: (b, 0, 0))],
        out_specs=pl.BlockSpec((1, rows, w), lambda b, l: (b, l, 0)),
        out_shape=jax.ShapeDtypeStruct((bsz, lp, w), F32),
        compiler_params=_cparams("parallel", "parallel"),
        name="mem_attention",
    )(q, mk, mv)


def _router_kernel(x_ref, g_ref, w_ref, b_ref, t_ref, route_ref, counts_ref, cnt_scr,
                   *, n_groups, per_group):
    @pl.when(pl.program_id(0) == 0)
    def _():
        cnt_scr[...] = jnp.zeros_like(cnt_scr)

    d = x_ref.shape[-1]
    t = _rms(x_ref[...], g_ref[...])
    t_ref[:, :d] = t.astype(BF16)
    logits = _dot(t, w_ref[...], HIGHEST) + b_ref[...]
    lane = lax.broadcasted_iota(jnp.int32, logits.shape, 1)

    def first_argmax(v):
        m = jnp.max(v, axis=-1, keepdims=True)
        return m, jnp.min(jnp.where(v == m, lane, LANES), axis=-1, keepdims=True)

    gl = jnp.where(lane < n_groups, logits, NEG_INF)
    gmax, gidx = first_argmax(gl)
    g_val = 1.0 / jnp.sum(jnp.exp(gl - gmax), axis=-1, keepdims=True)
    lo = n_groups + gidx * per_group
    el = jnp.where(lane >= lo, jnp.where(lane < lo + per_group, logits, NEG_INF), NEG_INF)
    m1, i1 = first_argmax(el)
    m2, i2 = first_argmax(jnp.where(lane == i1, NEG_INF, el))
    ex = jnp.exp(m2 - m1)
    den = 1.0 + ex
    gate1 = g_val * (1.0 / den)
    gate2 = g_val * (ex / den)

    j1 = i1 - lo
    j2 = i2 - lo
    cw = jnp.where(lane == j1, gate1, jnp.where(lane == j1 + per_group, gate1,
                   jnp.where(lane == j2, gate2, jnp.where(lane == j2 + per_group, gate2, 0.0))))
    hi = cw.astype(BF16)
    t_ref[:, d:] = jnp.where(lane < per_group, hi, (cw - hi.astype(F32)).astype(BF16))

    tm = logits.shape[0]
    picked = jnp.where(lane == gidx, 1.0, 0.0)
    earlier = (lax.broadcasted_iota(jnp.int32, (tm, tm), 0)
               > lax.broadcasted_iota(jnp.int32, (tm, tm), 1)).astype(BF16)
    before = cnt_scr[...] + _dot(earlier, picked.astype(BF16))
    rank = jnp.sum(jnp.where(lane == gidx, before, 0.0), axis=-1, keepdims=True).astype(jnp.int32)
    cnt_scr[...] = cnt_scr[...] + jnp.sum(picked, axis=0, keepdims=True)
    counts_ref[...] = cnt_scr[...]
    route_ref[...] = jnp.where(lane == 0, gidx, jnp.where(lane == 1, rank, 0))


def moe_router(x, g, w_rg, b_rg, w_re, b_re):
    t, d = x.shape
    n_groups, n_exp = w_rg.shape[1], w_re.shape[1]
    assert n_groups + n_exp <= LANES and t < 2 ** 24
    assert 2 * (n_exp // n_groups) <= LANES
    pad = LANES - n_groups - n_exp
    w = jnp.pad(jnp.concatenate([w_rg, w_re], axis=1), ((0, 0), (0, pad)))
    b = jnp.pad(jnp.concatenate([b_rg, b_re]), (0, pad)).reshape(1, LANES)
    tm = _token_tile(t)
    kern = functools.partial(_router_kernel, n_groups=n_groups, per_group=n_exp // n_groups)
    tn, route, counts = pl.pallas_call(
        kern,
        grid=(t // tm,),
        in_specs=[pl.BlockSpec((tm, d), lambda i: (i, 0)),
                  pl.BlockSpec((1, d), lambda i: (0, 0)),
                  pl.BlockSpec((d, LANES), lambda i: (0, 0)),
                  pl.BlockSpec((1, LANES), lambda i: (0, 0))],
        out_specs=[pl.BlockSpec((tm, d + LANES), lambda i: (i, 0)),
                   pl.BlockSpec((tm, LANES), lambda i: (i, 0)),
                   pl.BlockSpec((1, LANES), lambda i: (0, 0))],
        out_shape=[jax.ShapeDtypeStruct((t, d + LANES), BF16),
                   jax.ShapeDtypeStruct((t, LANES), jnp.int32),
                   jax.ShapeDtypeStruct((1, LANES), F32)],
        scratch_shapes=[pltpu.VMEM((1, LANES), F32)],
        compiler_params=_cparams("arbitrary"),
        name="moe_router",
    )(x, g.reshape(1, d), w, b)
    return tn, route, counts[0, :n_groups].astype(jnp.int32)


def _group_ffn_kernel(tg_ref, first_ref, nu_ref, xs_ref, wg_ref, wu_ref, wd_ref, ys_ref,
                      wg_scr, wu_scr, wd_scr, *, d, per_group):
    i = pl.program_id(0)

    @pl.when(first_ref[i] == 1)
    def _():
        wg_scr[...] = wg_ref[0].astype(BF16)
        wu_scr[...] = wu_ref[0].astype(BF16)
        wd_scr[...] = wd_ref[0].astype(BF16)

    @pl.when(i < nu_ref[0])
    def _():
        x = xs_ref[:, :d]
        pair = xs_ref[:, d:].astype(F32)
        cw = pair[:, :per_group] + pair[:, per_group:2 * per_group]
        acc = None
        for e in range(per_group):
            h = _silu(_dot(x, wg_scr[e])) * _dot(x, wu_scr[e])
            y = _dot((h * cw[:, e:e + 1]).astype(BF16), wd_scr[e])
            acc = y if acc is None else acc + y
        ys_ref[...] = acc

    @pl.when(i >= nu_ref[0])
    def _():
        ys_ref[...] = jnp.zeros_like(ys_ref)


def group_ffn(tile_group, first, n_used, xs, w_gate, w_up, w_down, layer, per_group, tile):
    n_slots, da = xs.shape
    d = da - LANES
    ff = w_gate.shape[-1]
    once = pl.Buffered(1)
    grid_spec = pltpu.PrefetchScalarGridSpec(
        num_scalar_prefetch=3,
        grid=(n_slots // tile,),
        in_specs=[pl.BlockSpec((tile, da), lambda i, tg, fi, nu: (i, 0)),
                  pl.BlockSpec((1, per_group, d, ff), lambda i, tg, fi, nu: (layer, tg[i], 0, 0),
                               pipeline_mode=once),
                  pl.BlockSpec((1, per_group, d, ff), lambda i, tg, fi, nu: (layer, tg[i], 0, 0),
                               pipeline_mode=once),
                  pl.BlockSpec((1, per_group, ff, d), lambda i, tg, fi, nu: (layer, tg[i], 0, 0),
                               pipeline_mode=once)],
        out_specs=pl.BlockSpec((tile, d), lambda i, tg, fi, nu: (i, 0)),
        scratch_shapes=[pltpu.VMEM((per_group, d, ff), BF16),
                        pltpu.VMEM((per_group, d, ff), BF16),
                        pltpu.VMEM((per_group, ff, d), BF16)],
    )
    return pl.pallas_call(
        functools.partial(_group_ffn_kernel, d=d, per_group=per_group),
        grid_spec=grid_spec,
        out_shape=jax.ShapeDtypeStruct((n_slots, d), F32),
        compiler_params=_cparams("arbitrary"),
        name="group_ffn",
    )(tile_group, first, n_used, xs, w_gate, w_up, w_down)


def _combine_kernel(x_ref, y_ref, g_ref, o_ref, *, final_norm):
    out = x_ref[...] + y_ref[...]
    if final_norm:
        out = _rms(out, g_ref[...])
    o_ref[...] = out


def moe_combine(x, y, g_final, final_norm):
    t, d = x.shape
    tm = _token_tile(t)
    row = pl.BlockSpec((tm, d), lambda i: (i, 0))
    return pl.pallas_call(
        functools.partial(_combine_kernel, final_norm=final_norm),
        grid=(t // tm,),
        in_specs=[row, row, pl.BlockSpec((1, d), lambda i: (0, 0))],
        out_specs=row,
        out_shape=jax.ShapeDtypeStruct((t, d), F32),
        compiler_params=_cparams("parallel"),
        name="moe_combine",
    )(x, y, g_final.reshape(1, d))


def hier_moe(x, g_ffn, w_rg, b_rg, w_re, b_re, w_gate, w_up, w_down, layer, g_final, final_norm):
    t, d = x.shape
    n_groups = w_rg.shape[1]
    per_group = w_re.shape[1] // n_groups
    tile = 256 if t >= 8192 else 64
    tn, route, counts = moe_router(x, g_ffn, w_rg, b_rg, w_re, b_re)
    grp, rank_t = route[:, 0], route[:, 1]
    groups = jnp.arange(n_groups, dtype=jnp.int32)
    starts = jnp.cumsum(counts) - counts
    padded = (counts + tile - 1) // tile * tile
    pend = jnp.cumsum(padded)
    pstart = pend - padded
    n_tiles = (t + n_groups * (tile - 1)) // tile
    n_slots = n_tiles * tile
    tile_start = jnp.arange(n_tiles, dtype=jnp.int32) * tile
    tile_group = jnp.minimum(jnp.sum(pend[None, :] <= tile_start[:, None], axis=1), n_groups - 1).astype(jnp.int32)
    first = jnp.concatenate([jnp.ones((1,), jnp.int32), (tile_group[1:] != tile_group[:-1]).astype(jnp.int32)])
    n_used = (pend[-1:] // tile).astype(jnp.int32)
    slot_of_token = jnp.sum(jnp.where(grp[:, None] == groups, pstart, 0), axis=-1) + rank_t

    order = jnp.argsort(grp)
    per_slot = lambda v: jnp.repeat(v[tile_group], tile)
    rank_s = jnp.arange(n_slots, dtype=jnp.int32) - per_slot(pstart)
    sorted_pos = jnp.minimum(per_slot(starts) + rank_s, t - 1)
    src = jnp.where(rank_s < per_slot(counts), order.at[sorted_pos].get(mode="promise_in_bounds"), 0)

    xs = tn.at[src].get(mode="promise_in_bounds")
    ys = group_ffn(tile_group, first, n_used, xs, w_gate, w_up, w_down, layer, per_group, tile)
    y = ys.at[slot_of_token].get(mode="promise_in_bounds")
    return moe_combine(x, y, g_final, final_norm)


def _latent_q_kernel(x_ref, gkv_ref, wdkv_ref, glat_ref, cos_ref, sin_ref, gmix_ref, winb_ref, gq_ref,
                     wuq_ref, wuk_ref, c_ref, kr_ref, cb_ref, krb_ref, qm_ref, ql_ref, qr_ref,
                     *, heads, kv, rd, nope, q_lora, scale):
    x = x_ref[...]
    xr = x * lax.rsqrt(jnp.mean(x * x, axis=-1, keepdims=True) + EPS)
    cos = cos_ref[...]
    sin = sin_ref[...]

    ckr = _dot((xr * gkv_ref[...]).astype(BF16), wdkv_ref[...])
    c = _rms(ckr[:, :kv], glat_ref[...])
    kr = ckr[:, kv:kv + rd] * cos + ckr[:, kv + LANES:kv + LANES + rd] * sin
    c_ref[...] = c
    kr_ref[...] = kr
    cb_ref[...] = c.astype(BF16)
    krb_ref[...] = kr.astype(BF16)

    proj = _dot((xr * gmix_ref[...]).astype(BF16), winb_ref[...])
    qm_ref[...] = proj[:, q_lora:]
    cq = _rms(proj[:, :q_lora], gq_ref[...]).astype(BF16)
    q = _dot(cq, wuq_ref[...])
    hw = nope + 2 * LANES
    for h in range(heads):
        qn = q[:, h * hw:h * hw + nope].astype(BF16)
        qrope = (q[:, h * hw + nope:h * hw + nope + rd] * cos
                 + q[:, h * hw + nope + LANES:h * hw + nope + LANES + rd] * sin)
        ql_ref[h] = (_dot(qn, wuk_ref[h]) * scale).astype(BF16)
        qr_ref[h] = (qrope * scale).astype(BF16)


def _swap_halves(w):
    half = w.shape[-1] // 2
    return jnp.concatenate([w[..., half:], w[..., :half]], axis=-1)


def latent_and_queries(x, g_kv_in, w_dkv, g_lat, cos_full, sin_signed, g_mix, w_in_b, g_q, w_uq, w_uk,
                       scale, table_blocks):
    t, d = x.shape
    kv = g_lat.shape[0]
    rd = w_dkv.shape[1] - kv
    _, heads, nope = w_uk.shape
    q_lora = g_q.shape[0]
    mem_w = w_in_b.shape[1] - q_lora
    assert rd <= LANES and kv % LANES == 0 and nope % LANES == 0 and q_lora % LANES == 0
    padl = lambda w: jnp.pad(w, ((0, 0), (0, LANES - rd)))
    w_r = w_dkv[:, kv:]
    wdkv_ext = jnp.concatenate([w_dkv[:, :kv], padl(w_r), padl(_swap_halves(w_r))], axis=1).astype(BF16)
    w_uq_h = w_uq.reshape(q_lora, heads, nope + rd)
    w_qr = w_uq_h[..., nope:]
    pad3 = lambda w: jnp.pad(w, ((0, 0), (0, 0), (0, LANES - rd)))
    wuq_ext = jnp.concatenate([w_uq_h[..., :nope], pad3(w_qr), pad3(_swap_halves(w_qr))], axis=-1)
    hw = nope + 2 * LANES
    wuq_ext = wuq_ext.reshape(q_lora, heads * hw).astype(BF16)
    wuk_t = jnp.transpose(w_uk, (1, 2, 0)).astype(BF16)
    tm = _token_tile(t, 256)
    nblk = table_blocks(tm)
    kern = functools.partial(_latent_q_kernel, heads=heads, kv=kv, rd=rd, nope=nope, q_lora=q_lora,
                             scale=scale)
    full = lambda shape: pl.BlockSpec(shape, lambda i: (0,) * len(shape))
    return pl.pallas_call(
        kern,
        grid=(t // tm,),
        in_specs=[pl.BlockSpec((tm, d), lambda i: (i, 0)),
                  full((1, d)), full(wdkv_ext.shape), full((1, kv)),
                  pl.BlockSpec((tm, rd), lambda i: (i % nblk, 0)),
                  pl.BlockSpec((tm, rd), lambda i: (i % nblk, 0)),
                  full((1, d)), full(w_in_b.shape), full((1, q_lora)),
                  full(wuq_ext.shape), full(wuk_t.shape)],
        out_specs=[pl.BlockSpec((tm, kv), lambda i: (i, 0)),
                   pl.BlockSpec((tm, rd), lambda i: (i, 0)),
                   pl.BlockSpec((tm, kv), lambda i: (i, 0)),
                   pl.BlockSpec((tm, rd), lambda i: (i, 0)),
                   pl.BlockSpec((tm, mem_w), lambda i: (i, 0)),
                   pl.BlockSpec((heads, tm, kv), lambda i: (0, i, 0)),
                   pl.BlockSpec((heads, tm, rd), lambda i: (0, i, 0))],
        out_shape=[jax.ShapeDtypeStruct((t, kv), F32),
                   jax.ShapeDtypeStruct((t, rd), F32),
                   jax.ShapeDtypeStruct((t, kv), BF16),
                   jax.ShapeDtypeStruct((t, rd), BF16),
                   jax.ShapeDtypeStruct((t, mem_w), F32),
                   jax.ShapeDtypeStruct((heads, t, kv), BF16),
                   jax.ShapeDtypeStruct((heads, t, rd), BF16)],
        compiler_params=_cparams("parallel"),
        name="latent_and_queries",
    )(x, g_kv_in.reshape(1, d), wdkv_ext, g_lat.reshape(1, kv), cos_full, sin_signed,
      g_mix.reshape(1, d), w_in_b.astype(BF16), g_q.reshape(1, q_lora), wuq_ext, wuk_t)


def _mla_prompt_kernel(qi_ref, kj_ref, ql_ref, qr_ref, c_ref, kr_ref, wuv_ref, o_ref,
                       m_scr, l_scr, acc_scr, *, heads, tq, vh, sub):
    p_idx = pl.program_id(1)
    qi = qi_ref[p_idx]
    kj = kj_ref[p_idx]
    kv = c_ref.shape[-1]

    @pl.when(kj == 0)
    def _():
        m_scr[...] = jnp.full_like(m_scr, NEG_INF)
        l_scr[...] = jnp.zeros_like(l_scr)
        acc_scr[...] = jnp.zeros_like(acc_scr)

    def step(masked):
        c = c_ref[...]
        kr = kr_ref[...]
        for h in range(heads):
            for r0 in range(0, tq, sub):
                rows = slice(h * tq + r0, h * tq + r0 + sub)
                s = _dot_nt(ql_ref[h, r0:r0 + sub, :], c) + _dot_nt(qr_ref[h, r0:r0 + sub, :], kr)
                if masked:
                    qpos = r0 + lax.broadcasted_iota(jnp.int32, (sub, tq), 0)
                    kpos = lax.broadcasted_iota(jnp.int32, (sub, tq), 1)
                    s = jnp.where(kpos <= qpos, s, NEG_INF)
                m_prev = m_scr[rows]
                m_new = jnp.maximum(m_prev, jnp.max(s, axis=-1, keepdims=True))
                alpha = jnp.exp(m_prev - m_new)
                p = jnp.exp(s - _lane_tile(m_new, tq // LANES))
                l_scr[rows] = alpha * l_scr[rows] + jnp.sum(p, axis=-1, keepdims=True)
                acc_scr[rows] = _lane_tile(alpha, kv // LANES) * acc_scr[rows] + _dot(p.astype(BF16), c)
                m_scr[rows] = m_new

    @pl.when(kj < qi)
    def _():
        step(False)

    @pl.when(kj == qi)
    def _():
        step(True)
        for h in range(heads):
            rows = slice(h * tq, (h + 1) * tq)
            out = acc_scr[rows] / _lane_tile(l_scr[rows], kv // LANES)
            o_ref[:, h * vh:(h + 1) * vh] = _dot(out.astype(BF16), wuv_ref[h])


def mla_prompt_attention(ql, qr, cb, krb, w_uv, bsz, seq, tq=512, sub=512):
    heads, t, kv = ql.shape
    rd = qr.shape[-1]
    vh = w_uv.shape[-1]
    tq = min(tq, seq)
    assert seq % tq == 0
    nq = seq // tq
    pairs = [(i, j) for i in range(nq) for j in range(i + 1)]
    qi = jnp.asarray([p[0] for p in pairs], jnp.int32)
    kj = jnp.asarray([p[1] for p in pairs], jnp.int32)
    wuv_h = jnp.transpose(w_uv, (1, 0, 2)).astype(BF16)
    grid_spec = pltpu.PrefetchScalarGridSpec(
        num_scalar_prefetch=2,
        grid=(bsz, len(pairs)),
        in_specs=[pl.BlockSpec((heads, tq, kv), lambda b, p, qi, kj: (0, b * nq + qi[p], 0)),
                  pl.BlockSpec((heads, tq, rd), lambda b, p, qi, kj: (0, b * nq + qi[p], 0)),
                  pl.BlockSpec((tq, kv), lambda b, p, qi, kj: (b * nq + kj[p], 0)),
                  pl.BlockSpec((tq, rd), lambda b, p, qi, kj: (b * nq + kj[p], 0)),
                  pl.BlockSpec((heads, kv, vh), lambda b, p, qi, kj: (0, 0, 0))],
        out_specs=pl.BlockSpec((tq, heads * vh), lambda b, p, qi, kj: (b * nq + qi[p], 0)),
        scratch_shapes=[pltpu.VMEM((heads * tq, LANES), F32),
                        pltpu.VMEM((heads * tq, LANES), F32),
                        pltpu.VMEM((heads * tq, kv), F32)],
    )
    sub = min(sub, tq)
    assert tq % sub == 0 and tq % LANES == 0 and kv % LANES == 0
    return pl.pallas_call(
        functools.partial(_mla_prompt_kernel, heads=heads, tq=tq, vh=vh, sub=sub),
        grid_spec=grid_spec,
        out_shape=jax.ShapeDtypeStruct((t, heads * vh), F32),
        compiler_params=_cparams("parallel", "arbitrary"),
        name="mla_prompt_attention",
    )(qi, kj, ql, qr, cb, krb, wuv_h)


def _mla_paged_kernel(pt_ref, ql_ref, qr_ref, cn_ref, krn_ref, clat_hbm, ckr_hbm, wuv_ref, o_ref,
                      kbuf, rbuf, sems, m_scr, l_scr, acc_scr,
                      *, heads, n_new, head_slots, page, pages_per_chunk, chunks_per_seq, vh):
    g = pl.program_id(0)
    total = pl.num_programs(0)
    ci = g % chunks_per_seq
    slot = g % 2

    def page_copies(chunk, slot_, p):
        pg = pt_ref[chunk * pages_per_chunk + p]
        rows = pl.ds(p * page, page)
        return (pltpu.make_async_copy(clat_hbm.at[pg], kbuf.at[slot_, rows], sems.at[0, slot_]),
                pltpu.make_async_copy(ckr_hbm.at[pg], rbuf.at[slot_, :, rows], sems.at[1, slot_]))

    def start_chunk(chunk, slot_):
        for p in range(pages_per_chunk):
            for cp in page_copies(chunk, slot_, p):
                cp.start()

    @pl.when(g == 0)
    def _():
        start_chunk(0, 0)

    @pl.when(g + 1 < total)
    def _():
        start_chunk(g + 1, 1 - slot)

    for p in range(pages_per_chunk):
        for cp in page_copies(g, slot, p):
            cp.wait()

    @pl.when(ci == 0)
    def _():
        m_scr[...] = jnp.full_like(m_scr, NEG_INF)
        l_scr[...] = jnp.zeros_like(l_scr)
        acc_scr[...] = jnp.zeros_like(acc_scr)

    ql = ql_ref[0]
    qr = qr_ref[0]

    def update(s, v):
        m_prev = m_scr[...]
        m_new = jnp.maximum(m_prev, jnp.max(s, axis=-1, keepdims=True))
        alpha = jnp.exp(m_prev - m_new)
        p = jnp.exp(s - m_new)
        l_scr[...] = alpha * l_scr[...] + jnp.sum(p, axis=-1, keepdims=True)
        acc_scr[...] = alpha * acc_scr[...] + _dot(p.astype(BF16), v)
        m_scr[...] = m_new

    k = kbuf[slot].astype(BF16)
    update(_dot_nt(ql, k) + _dot(qr, rbuf[slot].astype(BF16)), k)

    @pl.when(ci == chunks_per_seq - 1)
    def _():
        cn = cn_ref[0].astype(BF16)
        s = _dot_nt(ql, cn) + _dot_nt(qr, krn_ref[0].astype(BF16))
        qpos = lax.broadcasted_iota(jnp.int32, s.shape, 0) // head_slots
        kpos = lax.broadcasted_iota(jnp.int32, s.shape, 1)
        update(jnp.where(kpos <= qpos, s, NEG_INF), cn)
        out = (acc_scr[...] / l_scr[...]).astype(BF16)
        full = jnp.concatenate([_dot(out, wuv_ref[h]) for h in range(heads)], axis=-1)
        row_head = lax.broadcasted_iota(jnp.int32, full.shape, 0) % head_slots
        lane_head = lax.broadcasted_iota(jnp.int32, full.shape, 1) // vh
        own = jnp.where(row_head == lane_head, full, 0.0)
        o_ref[0] = jnp.sum(own.reshape(n_new, head_slots, heads * vh), axis=1)


def mla_paged_attention(page_table, ql, qr, c_new, kr_new, cache_lat, cache_kr, w_uv, head_slots):
    bsz, n_pages = page_table.shape
    _, page, kv = cache_lat.shape
    rd = cache_kr.shape[-1]
    q_rows = ql.shape[1]
    n_new = c_new.shape[1]
    heads, vh = w_uv.shape[1], w_uv.shape[2]
    pages_per_chunk = min(n_pages, 32)
    assert n_pages % pages_per_chunk == 0
    chunks_per_seq = n_pages // pages_per_chunk
    chunk_rows = pages_per_chunk * page
    new_pad = -n_new % SUBLANES
    c_new = jnp.pad(c_new, ((0, 0), (0, new_pad), (0, 0)))
    kr_new = jnp.pad(kr_new, ((0, 0), (0, new_pad), (0, 0)))
    wuv_h = jnp.transpose(w_uv, (1, 0, 2)).astype(BF16)
    cache_kr_t = jnp.transpose(cache_kr, (0, 2, 1))
    assert q_rows == n_new * head_slots and head_slots % SUBLANES == 0 and heads <= head_slots
    kern = functools.partial(_mla_paged_kernel, heads=heads, n_new=n_new, head_slots=head_slots, page=page,
                             pages_per_chunk=pages_per_chunk, chunks_per_seq=chunks_per_seq, vh=vh)
    seq_of = lambda g, pt: (g // chunks_per_seq, 0, 0)
    grid_spec = pltpu.PrefetchScalarGridSpec(
        num_scalar_prefetch=1,
        grid=(bsz * chunks_per_seq,),
        in_specs=[pl.BlockSpec((1, q_rows, kv), seq_of),
                  pl.BlockSpec((1, q_rows, rd), seq_of),
                  pl.BlockSpec((1, n_new + new_pad, kv), seq_of),
                  pl.BlockSpec((1, n_new + new_pad, rd), seq_of),
                  pl.BlockSpec(memory_space=pl.ANY),
                  pl.BlockSpec(memory_space=pl.ANY),
                  pl.BlockSpec((heads, kv, vh), lambda g, pt: (0, 0, 0))],
        out_specs=pl.BlockSpec((1, n_new, heads * vh), seq_of),
        scratch_shapes=[pltpu.VMEM((2, chunk_rows, kv), F32),
                        pltpu.VMEM((2, rd, chunk_rows), F32),
                        pltpu.SemaphoreType.DMA((2, 2)),
                        pltpu.VMEM((q_rows, 1), F32),
                        pltpu.VMEM((q_rows, 1), F32),
                        pltpu.VMEM((q_rows, kv), F32)],
    )
    return pl.pallas_call(
        kern,
        grid_spec=grid_spec,
        out_shape=jax.ShapeDtypeStruct((bsz, n_new, heads * vh), F32),
        compiler_params=_cparams("arbitrary"),
        name="mla_paged_attention",
    )(page_table.reshape(-1), ql, qr, c_new, kr_new, cache_lat, cache_kr_t, wuv_h)


def _rope_tables(pos, rd):
    inv = ROPE_THETA ** (-jnp.arange(0, rd, 2, dtype=F32) / rd)
    ang = pos.astype(F32)[:, None] * inv[None, :]
    cos, sin = jnp.cos(ang), jnp.sin(ang)
    return jnp.concatenate([cos, cos], axis=-1), jnp.concatenate([-sin, sin], axis=-1)


def _run_group(x3, pos, mem_k, mem_v, conv_buf, d_state, paged, p):
    bsz, seq, d = x3.shape
    t = bsz * seq
    x = x3.reshape(t, d)
    heads_dn, dk, dv = d_state.shape[1:]
    conv_ch = p["conv_w"].shape[-1]
    v_w = heads_dn * dv
    mem_heads = mem_k.shape[3]
    mem_w = mem_heads * mem_k.shape[4]
    mem_tokens = mem_k.shape[2]

    w_in = p["w_in_a"][0]
    w_ab = jnp.pad(w_in[:, conv_ch + v_w:conv_ch + v_w + 2 * heads_dn], ((0, 0), (0, LANES - 2 * heads_dn)))
    w_in_r = jnp.concatenate([w_in[:, :conv_ch + v_w], w_in[:, conv_ch + v_w + 2 * heads_dn:], w_ab],
                             axis=1).astype(BF16)
    qkv, z, qm, ab = norm_matmul(x, p["g_mix"][0], w_in_r, (conv_ch, v_w, mem_w, LANES), (F32,) * 4)
    rows = DELTA_CHUNK if seq % DELTA_CHUNK == 0 else -(-seq // SUBLANES) * SUBLANES
    seq_p = -(-seq // rows) * rows
    seq3 = lambda a: jnp.pad(a.reshape(bsz, seq, -1), ((0, 0), (0, seq_p - seq), (0, 0)))
    o_dn, new_buf, new_state = delta_mixer(seq3(qkv), seq3(ab), seq3(z), conv_buf, d_state, p["conv_w"][0],
                                           p["a_log"][0], p["dt_bias"][0], p["g_onorm"][0],
                                           rows=rows, n_valid=min(seq, rows),
                                           nb=math.gcd(bsz, 2 if rows >= DELTA_CHUNK else 4))
    o_dn = o_dn[:, :seq].reshape(t, v_w)
    mo = mem_attention(seq3(qm), mem_k[0].reshape(bsz, mem_tokens, mem_w),
                       mem_v[0].reshape(bsz, mem_tokens, mem_w), mem_heads)[:, :seq].reshape(t, mem_w)
    x = proj_residual(o_dn, mo, p["w_out_a"][0].astype(BF16), x)
    x = hier_moe(x, p["g_ffn"][0], p["w_router_group"][0], p["b_router_group"][0], p["w_router_expert"][0],
                 p["b_router_expert"][0], p["w_exp_gate"], p["w_exp_up"], p["w_exp_down"], 0,
                 p["g_final"], False)

    kv = p["g_kv_latent"].shape[0]
    rd = p["w_dkv"].shape[1] - kv
    nope = p["w_uk"].shape[2]
    mla_heads = p["w_uk"].shape[1]
    scale = (nope + rd) ** -0.5
    if paged is None:
        cos_t, sin_t = _rope_tables(pos, rd)
        table_blocks = lambda tm: seq // tm
    else:
        cos_t, sin_t = _rope_tables(jnp.tile(pos, bsz), rd)
        table_blocks = lambda tm: t // tm
    c, kr, cb, krb, qm, ql, qr = latent_and_queries(
        x, p["g_kv_in"], p["w_dkv"], p["g_kv_latent"], cos_t, sin_t, p["g_mix"][1], p["w_in_b"][0],
        p["g_q"][0], p["w_uq"][0], p["w_uk"], scale, table_blocks)
    if paged is None:
        o_mla = mla_prompt_attention(ql, qr, cb, krb, p["w_uv"], bsz, seq)
    else:
        head_slots = -(-mla_heads // SUBLANES) * SUBLANES
        to_rows = lambda q: jnp.pad(
            jnp.transpose(q.reshape(mla_heads, bsz, seq, -1), (1, 2, 0, 3)),
            ((0, 0), (0, 0), (0, head_slots - mla_heads), (0, 0))).reshape(bsz, seq * head_slots, -1)
        o_mla = mla_paged_attention(paged[2], to_rows(ql), to_rows(qr), c.reshape(bsz, seq, kv),
                                    kr.reshape(bsz, seq, rd), paged[0], paged[1], p["w_uv"],
                                    head_slots).reshape(t, -1)
    mo = mem_attention(seq3(qm), mem_k[1].reshape(bsz, mem_tokens, mem_w),
                       mem_v[1].reshape(bsz, mem_tokens, mem_w), mem_heads)[:, :seq].reshape(t, mem_w)
    x = proj_residual(o_mla, mo, p["w_out_b"][0].astype(BF16), x)
    y = hier_moe(x, p["g_ffn"][1], p["w_router_group"][1], p["b_router_group"][1], p["w_router_expert"][1],
                 p["b_router_expert"][1], p["w_exp_gate"], p["w_exp_up"], p["w_exp_down"], 1,
                 p["g_final"], True)
    return (y.reshape(bsz, seq, d), c.reshape(bsz, seq, kv), kr.reshape(bsz, seq, rd),
            new_buf[None], new_state[None])


def kernel(x_prompt, x_sample, mem_prompt, state_delta, cache_conv, cache_kv_latent, cache_k_rope, cache_mem_k, cache_mem_v, page_table, g_mix, g_ffn, g_final, w_in_a, conv_w, a_log, dt_bias, g_onorm, w_out_a, g_kv_in, w_dkv, g_kv_latent, w_uk, w_uv, w_in_b, g_q, w_uq, w_out_b, g_mem, w_mem_k, w_mem_v, w_router_group, b_router_group, w_router_expert, b_router_expert, w_exp_gate, w_exp_up, w_exp_down):
    p = dict(g_mix=g_mix, g_ffn=g_ffn, g_final=g_final, w_in_a=w_in_a, conv_w=conv_w, a_log=a_log,
             dt_bias=dt_bias, g_onorm=g_onorm, w_out_a=w_out_a, g_kv_in=g_kv_in, w_dkv=w_dkv,
             g_kv_latent=g_kv_latent, w_uk=w_uk, w_uv=w_uv, w_in_b=w_in_b, g_q=g_q, w_uq=w_uq,
             w_out_b=w_out_b, w_router_group=w_router_group, b_router_group=b_router_group,
             w_router_expert=w_router_expert, b_router_expert=b_router_expert, w_exp_gate=w_exp_gate,
             w_exp_up=w_exp_up, w_exp_down=w_exp_down)
    depth = g_mix.shape[0]
    assert depth == 2 and w_in_a.shape[0] == 1
    bp, lp, d = x_prompt.shape
    bs, ls, _ = x_sample.shape
    mem_tokens = mem_prompt.shape[1]
    mem_heads, mem_hd = cache_mem_k.shape[3], cache_mem_k.shape[4]
    mem_w = mem_heads * mem_hd

    mem_flat = mem_prompt.reshape(bp * mem_tokens, d)
    mkv = [norm_matmul(mem_flat, g_mem[l], jnp.concatenate([w_mem_k[l], w_mem_v[l]], axis=1).astype(BF16),
                       (mem_w, mem_w), (F32, F32)) for l in range(depth)]
    shape5 = (bp, mem_tokens, mem_heads, mem_hd)
    mem_k_prompt = jnp.stack([m[0].reshape(shape5) for m in mkv])
    mem_v_prompt = jnp.stack([m[1].reshape(shape5) for m in mkv])
    conv0 = jnp.zeros((bp,) + cache_conv.shape[2:], F32)
    state0 = jnp.zeros((bp,) + state_delta.shape[2:], F32)
    y_p, c_p, kr_p, conv_p, state_p = _run_group(
        x_prompt, jnp.arange(lp), mem_k_prompt, mem_v_prompt, conv0, state0, None, p)

    past_len = page_table.shape[1] * cache_kv_latent.shape[1]
    y_s, c_s, kr_s, conv_s, state_s = _run_group(
        x_sample, past_len + jnp.arange(ls), cache_mem_k, cache_mem_v, cache_conv[0], state_delta[0],
        (cache_kv_latent, cache_k_rope, page_table), p)

    return (y_p, y_s, c_p, kr_p, c_s, kr_s, state_p, conv_p, state_s, conv_s, mem_k_prompt, mem_v_prompt)
```

```python
import functools
import math

import jax
import jax.numpy as jnp
from jax import lax
from jax.experimental import pallas as pl
from jax.experimental.pallas import tpu as pltpu

F32 = jnp.float32
BF16 = jnp.bfloat16
EPS = 1e-6
ROPE_THETA = 10000.0
TOP_K = 2
DELTA_CHUNK = 64
LANES = 128
SUBLANES = 8
VMEM_LIMIT = 56 * 1024 * 1024
NEG_INF = float("-inf")
HIGHEST = lax.Precision.HIGHEST

NT_DIMS = (((1,), (1,)), ((), ()))
TN_DIMS = (((0,), (0,)), ((), ()))


def _cparams(*sem):
    return pltpu.CompilerParams(dimension_semantics=sem, vmem_limit_bytes=VMEM_LIMIT)


def _rms(x, g):
    return x * lax.rsqrt(jnp.mean(x * x, axis=-1, keepdims=True) + EPS) * g


def _silu(x):
    return x * jax.nn.sigmoid(x)


def _dot(a, b, precision=None):
    return jnp.dot(a, b, preferred_element_type=F32, precision=precision)


def _dot_nt(a, b, precision=None):
    return lax.dot_general(a, b, NT_DIMS, preferred_element_type=F32, precision=precision)


def _lane_tile(x, k):
    return jnp.concatenate([x] * k, axis=1)


def _token_tile(t, cap=512):
    tm = min(t, cap)
    assert t % tm == 0
    return tm


def _norm_matmul_kernel(x_ref, g_ref, w_ref, *out_refs, splits):
    xn = _rms(x_ref[...], g_ref[...]).astype(BF16)
    off = 0
    for o_ref, n in zip(out_refs, splits):
        o_ref[...] = _dot(xn, w_ref[:, off:off + n]).astype(o_ref.dtype)
        off += n


def norm_matmul(x, g, w_bf16, splits, out_dtypes):
    t, d = x.shape
    tm = _token_tile(t)
    n = w_bf16.shape[1]
    assert sum(splits) == n and all(s % LANES == 0 for s in splits)
    return pl.pallas_call(
        functools.partial(_norm_matmul_kernel, splits=tuple(splits)),
        grid=(t // tm,),
        in_specs=[pl.BlockSpec((tm, d), lambda i: (i, 0)),
                  pl.BlockSpec((1, d), lambda i: (0, 0)),
                  pl.BlockSpec((d, n), lambda i: (0, 0))],
        out_specs=[pl.BlockSpec((tm, s), lambda i: (i, 0)) for s in splits],
        out_shape=[jax.ShapeDtypeStruct((t, s), dt) for s, dt in zip(splits, out_dtypes)],
        compiler_params=_cparams("parallel"),
        name="norm_matmul",
    )(x, g.reshape(1, d), w_bf16)


def _proj_residual_kernel(a_ref, b_ref, wa_ref, wb_ref, res_ref, o_ref):
    acc = _dot(a_ref[...].astype(BF16), wa_ref[...]) + _dot(b_ref[...].astype(BF16), wb_ref[...])
    o_ref[...] = res_ref[...] + acc


def proj_residual(a, b, w_bf16, res):
    t, d = res.shape
    ka, kb = a.shape[1], b.shape[1]
    tm = _token_tile(t)
    wa, wb = w_bf16[:ka], w_bf16[ka:]
    return pl.pallas_call(
        _proj_residual_kernel,
        grid=(t // tm,),
        in_specs=[pl.BlockSpec((tm, ka), lambda i: (i, 0)),
                  pl.BlockSpec((tm, kb), lambda i: (i, 0)),
                  pl.BlockSpec((ka, d), lambda i: (0, 0)),
                  pl.BlockSpec((kb, d), lambda i: (0, 0)),
                  pl.BlockSpec((tm, d), lambda i: (i, 0))],
        out_specs=pl.BlockSpec((tm, d), lambda i: (i, 0)),
        out_shape=jax.ShapeDtypeStruct((t, d), F32),
        compiler_params=_cparams("parallel"),
        name="proj_residual",
    )(a, b, wa, wb, res)


def _delta_kernel(qkv_ref, ab_ref, z_ref, cbuf_ref, s0_ref, cw_ref, alog_ref, dtb_ref, gon_ref,
                  o_ref, nbuf_ref, sout_ref, xp_scr, s_scr, *, rows, n_valid, nb, heads, dk, dv, n_taps):
    li = pl.program_id(1)
    halo = n_taps - 1
    base = SUBLANES
    qk_w = heads * dk

    @pl.when(li == 0)
    def _():
        s_scr[...] = s0_ref[...]
        xp_scr[:, base - halo:base, :] = cbuf_ref[...]

    xp_scr[:, base:base + rows, :] = qkv_ref[...]

    row = lax.broadcasted_iota(jnp.int32, (rows, rows), 0)
    col = lax.broadcasted_iota(jnp.int32, (rows, rows), 1)
    incl = row >= col
    strict = row > col
    eye = (row == col).astype(F32)
    n_double = max(0, int(math.ceil(math.log2(rows))) - 1)

    ys, beta_all, gcum, gcum_t, exp_g, exp_rem, exp_last = ([] for _ in range(7))
    for b in range(nb):
        acc = cw_ref[0:1, :] * xp_scr[b, base - halo:base - halo + rows, :]
        for j in range(1, n_taps):
            acc = acc + cw_ref[j:j + 1, :] * xp_scr[b, base - halo + j:base - halo + j + rows, :]
        ys.append(_silu(acc))
        tail = xp_scr[b, base + n_valid - halo:base + n_valid, :]
        xp_scr[b, base - halo:base, :] = tail
        nbuf_ref[b] = tail

        ab = ab_ref[b]
        sp = jnp.maximum(ab + dtb_ref[...], 0.0) + jnp.log1p(jnp.exp(-jnp.abs(ab + dtb_ref[...])))
        g_all = -jnp.exp(alog_ref[...]) * sp
        bt = jax.nn.sigmoid(ab)
        if n_valid < rows:
            valid = lax.broadcasted_iota(jnp.int32, (rows, LANES), 0) < n_valid
            g_all = jnp.where(valid, g_all, 0.0)
            bt = jnp.where(valid, bt, 0.0)
        beta_all.append(bt)
        gc = _dot(incl.astype(F32), g_all, HIGHEST)
        gcum.append(gc)
        gcum_t.append(lax.dot_general(g_all, (col >= row).astype(F32), TN_DIMS, precision=HIGHEST,
                                      preferred_element_type=F32))
        glast = gc[rows - 1:rows, :]
        exp_g.append(jnp.exp(gc))
        exp_rem.append(jnp.exp(glast - gc))
        exp_last.append(jnp.exp(glast))

    units = [(b, h) for b in range(nb) for h in range(heads)]
    qn, kn, vb, gamma, beta, eg, er, el, kb, qb = ({} for _ in range(10))
    for b, h in units:
        y, u_ = ys[b], (b, h)
        qh = y[:, h * dk:(h + 1) * dk]
        kh = y[:, qk_w + h * dk:qk_w + (h + 1) * dk]
        qn[u_] = qh * lax.rsqrt(jnp.sum(qh * qh, axis=-1, keepdims=True) + EPS) * dk ** -0.5
        kn[u_] = kh * lax.rsqrt(jnp.sum(kh * kh, axis=-1, keepdims=True) + EPS)
        beta[u_] = beta_all[b][:, heads + h:heads + h + 1]
        vb[u_] = y[:, 2 * qk_w + h * dv:2 * qk_w + (h + 1) * dv] * beta[u_]
        eg[u_] = exp_g[b][:, h:h + 1]
        er[u_] = exp_rem[b][:, h:h + 1]
        el[u_] = exp_last[b][:, h:h + 1]
        gamma[u_] = jnp.exp(jnp.where(incl, gcum[b][:, h:h + 1] - gcum_t[b][h:h + 1, :], NEG_INF))
        kb[u_] = kn[u_].astype(BF16)
        qb[u_] = qn[u_].astype(BF16)

    kk = {u_: _dot_nt(kb[u_], kb[u_]) for u_ in units}
    qk = {u_: _dot_nt(qb[u_], kb[u_]) for u_ in units}
    pw = {u_: -jnp.where(strict, kk[u_] * gamma[u_] * beta[u_], 0.0) for u_ in units}
    tinv = {u_: eye + pw[u_] for u_ in units}
    for _ in range(n_double):
        pwb = {u_: pw[u_].astype(BF16) for u_ in units}
        pw = {u_: _dot(pwb[u_], pwb[u_]) for u_ in units}
        tinv = {u_: tinv[u_] + _dot(tinv[u_].astype(BF16), pw[u_].astype(BF16)) for u_ in units}
    sol = {u_: _dot(tinv[u_].astype(BF16),
                    jnp.concatenate([vb[u_], kn[u_] * (beta[u_] * eg[u_])], axis=-1).astype(BF16))
           for u_ in units}

    s_old = {(b, h): s_scr[b, h] for b, h in units}
    sb = {u_: s_old[u_].astype(BF16) for u_ in units}
    ws = {u_: _dot(sol[u_][:, dv:].astype(BF16), sb[u_]) for u_ in units}
    qs = {u_: _dot((qn[u_] * eg[u_]).astype(BF16), sb[u_]) for u_ in units}
    eb = {u_: (sol[u_][:, :dv] - ws[u_]).astype(BF16) for u_ in units}
    o = {u_: qs[u_] + _dot((qk[u_] * gamma[u_]).astype(BF16), eb[u_]) for u_ in units}
    for b, h in units:
        u_ = (b, h)
        s_scr[b, h] = s_old[u_] * el[u_] + lax.dot_general((kn[u_] * er[u_]).astype(BF16), eb[u_], TN_DIMS,
                                                           preferred_element_type=F32)
        zh = z_ref[b, :, h * dv:(h + 1) * dv]
        o_ref[b, :, h * dv:(h + 1) * dv] = (_rms(o[u_], gon_ref[...]) * _silu(zh)).astype(o_ref.dtype)

    @pl.when(li == pl.num_programs(1) - 1)
    def _():
        sout_ref[...] = s_scr[...]


def delta_mixer(qkv, ab, z, conv_buf, s0, conv_w, a_log, dt_bias, g_onorm, *, rows, n_valid, nb):
    bsz, lp, ch = qkv.shape
    _, heads, dk, dv = s0.shape
    n_taps = conv_w.shape[0]
    n_blk = lp // rows
    assert lp % rows == 0 and (n_valid == rows or n_blk == 1) and bsz % nb == 0 and n_valid >= n_taps - 1
    pad_h = lambda v: jnp.pad(v.astype(F32), (0, LANES - heads)).reshape(1, LANES)
    kern = functools.partial(_delta_kernel, rows=rows, n_valid=n_valid, nb=nb, heads=heads, dk=dk, dv=dv,
                             n_taps=n_taps)
    return pl.pallas_call(
        kern,
        grid=(bsz // nb, n_blk),
        in_specs=[pl.BlockSpec((nb, rows, ch), lambda b, l: (b, l, 0)),
                  pl.BlockSpec((nb, rows, LANES), lambda b, l: (b, l, 0)),
                  pl.BlockSpec((nb, rows, heads * dv), lambda b, l: (b, l, 0)),
                  pl.BlockSpec((nb, n_taps - 1, ch), lambda b, l: (b, 0, 0)),
                  pl.BlockSpec((nb, heads, dk, dv), lambda b, l: (b, 0, 0, 0)),
                  pl.BlockSpec((n_taps, ch), lambda b, l: (0, 0)),
                  pl.BlockSpec((1, LANES), lambda b, l: (0, 0)),
                  pl.BlockSpec((1, LANES), lambda b, l: (0, 0)),
                  pl.BlockSpec((1, dv), lambda b, l: (0, 0))],
        out_specs=[pl.BlockSpec((nb, rows, heads * dv), lambda b, l: (b, l, 0)),
                   pl.BlockSpec((nb, n_taps - 1, ch), lambda b, l: (b, 0, 0)),
                   pl.BlockSpec((nb, heads, dk, dv), lambda b, l: (b, 0, 0, 0))],
        out_shape=[jax.ShapeDtypeStruct((bsz, lp, heads * dv), BF16),
                   jax.ShapeDtypeStruct((bsz, n_taps - 1, ch), F32),
                   jax.ShapeDtypeStruct((bsz, heads, dk, dv), F32)],
        scratch_shapes=[pltpu.VMEM((nb, SUBLANES + rows, ch), F32),
                        pltpu.VMEM((nb, heads, dk, dv), F32)],
        compiler_params=_cparams("parallel", "arbitrary"),
        name="delta_mixer",
    )(qkv, ab, z, conv_buf, s0, conv_w, pad_h(a_log), pad_h(dt_bias), g_onorm.reshape(1, dv))


def _mem_attn_kernel(q_ref, k_ref, v_ref, o_ref, *, nb, heads, hd, feature_major):
    units = [(b, h) for b in range(nb) for h in range(heads)]
    lane_head = lax.broadcasted_iota(jnp.int32, q_ref.shape[1:], 1) // hd
    k = [k_ref[b].astype(BF16) for b in range(nb)]
    v = [v_ref[b].astype(BF16) for b in range(nb)]
    qh = {(b, h): jnp.where(lane_head == h, q_ref[b], 0.0).astype(BF16) for b, h in units}
    if feature_major:
        s = {(b, h): _dot(qh[b, h], k[b]) * hd ** -0.5 for b, h in units}
    else:
        s = {(b, h): _dot_nt(qh[b, h], k[b]) * hd ** -0.5 for b, h in units}
    p = {u: jnp.exp(s[u] - jnp.max(s[u], axis=-1, keepdims=True)) for u in units}
    p = {u: (p[u] / jnp.sum(p[u], axis=-1, keepdims=True)).astype(BF16) for u in units}
    if feature_major:
        o = {(b, h): _dot_nt(p[b, h], v[b]) for b, h in units}
    else:
        o = {(b, h): _dot(p[b, h], v[b]) for b, h in units}
    for b in range(nb):
        out = jnp.where(lane_head == 0, o[b, 0], 0.0)
        for h in range(1, heads):
            out = out + jnp.where(lane_head == h, o[b, h], 0.0)
        o_ref[b] = out


def mem_attention(q, mk, mv, heads, feature_major):
    bsz, lp, w = q.shape
    rows = min(lp, 512)
    nb = math.gcd(bsz, max(1, 64 // rows))
    assert lp % rows == 0
    kv_block = (nb,) + mk.shape[1:]
    return pl.pallas_call(
        functools.partial(_mem_attn_kernel, nb=nb, heads=heads, hd=w // heads, feature_major=feature_major),
        grid=(bsz // nb, lp // rows),
        in_specs=[pl.BlockSpec((nb, rows, w), lambda b, l: (b, l, 0)),
                  pl.BlockSpec(kv_block, lambda b, l: (b, 0, 0)),
                  pl.BlockSpec(kv_block, lambda b, l: (b, 0, 0))],
        out_specs=pl.BlockSpec((nb, rows, w), lambda b, l: (b, l, 0)),
        out_shape=jax.ShapeDtypeStruct((bsz, lp, w), F32),
        compiler_params=_cparams("parallel", "parallel"),
        name="mem_attention",
    )(q, mk, mv)


def _router_kernel(x_ref, g_ref, whi_ref, wlo_ref, b_ref, t_ref, route_ref, counts_ref, cnt_scr, earlier_scr,
                   *, n_groups, per_group):
    tm, d = x_ref.shape

    @pl.when(pl.program_id(0) == 0)
    def _():
        cnt_scr[...] = jnp.zeros_like(cnt_scr)
        earlier_scr[...] = (lax.broadcasted_iota(jnp.int32, (tm, tm), 0)
                            > lax.broadcasted_iota(jnp.int32, (tm, tm), 1)).astype(BF16)

    t = _rms(x_ref[...], g_ref[...])
    t_hi = t.astype(BF16)
    t_ref[:, :d] = t_hi
    t_lo = (t - t_hi.astype(F32)).astype(BF16)
    logits = (_dot(t_hi, whi_ref[...]) + (_dot(t_lo, whi_ref[...]) + _dot(t_hi, wlo_ref[...]))
              + b_ref[...])
    lane = lax.broadcasted_iota(jnp.int32, logits.shape, 1)

    def first_argmax(v):
        m = jnp.max(v, axis=-1, keepdims=True)
        return m, jnp.min(jnp.where(v == m, lane, LANES), axis=-1, keepdims=True)

    gl = jnp.where(lane < n_groups, logits, NEG_INF)
    gmax, gidx = first_argmax(gl)
    g_val = 1.0 / jnp.sum(jnp.exp(gl - gmax), axis=-1, keepdims=True)
    lo = n_groups + gidx * per_group
    el = jnp.where(lane >= lo, jnp.where(lane < lo + per_group, logits, NEG_INF), NEG_INF)
    m1, i1 = first_argmax(el)
    m2, i2 = first_argmax(jnp.where(lane == i1, NEG_INF, el))
    ex = jnp.exp(m2 - m1)
    den = 1.0 + ex
    gate1 = g_val * (1.0 / den)
    gate2 = g_val * (ex / den)

    j1 = i1 - lo
    j2 = i2 - lo
    cw = jnp.where(lane == j1, gate1, jnp.where(lane == j1 + per_group, gate1,
                   jnp.where(lane == j2, gate2, jnp.where(lane == j2 + per_group, gate2, 0.0))))
    hi = cw.astype(BF16)
    t_ref[:, d:] = jnp.where(lane < per_group, hi, (cw - hi.astype(F32)).astype(BF16))

    picked = jnp.where(lane == gidx, 1.0, 0.0)
    before = cnt_scr[...] + _dot(earlier_scr[...], picked.astype(BF16))
    rank = jnp.sum(jnp.where(lane == gidx, before, 0.0), axis=-1, keepdims=True).astype(jnp.int32)
    cnt_scr[...] = cnt_scr[...] + jnp.sum(picked, axis=0, keepdims=True)
    counts_ref[...] = cnt_scr[...]
    route_ref[...] = jnp.where(lane == 0, gidx, jnp.where(lane == 1, rank, 0))


def moe_router(x, g, w_rg, b_rg, w_re, b_re):
    t, d = x.shape
    n_groups, n_exp = w_rg.shape[1], w_re.shape[1]
    assert n_groups + n_exp <= LANES and t < 2 ** 24
    assert 2 * (n_exp // n_groups) <= LANES
    pad = LANES - n_groups - n_exp
    w = jnp.pad(jnp.concatenate([w_rg, w_re], axis=1), ((0, 0), (0, pad)))
    b = jnp.pad(jnp.concatenate([b_rg, b_re]), (0, pad)).reshape(1, LANES)
    w_hi = w.astype(BF16)
    w_lo = (w - w_hi.astype(F32)).astype(BF16)
    tm = _token_tile(t)
    kern = functools.partial(_router_kernel, n_groups=n_groups, per_group=n_exp // n_groups)
    tn, route, counts = pl.pallas_call(
        kern,
        grid=(t // tm,),
        in_specs=[pl.BlockSpec((tm, d), lambda i: (i, 0)),
                  pl.BlockSpec((1, d), lambda i: (0, 0)),
                  pl.BlockSpec((d, LANES), lambda i: (0, 0)),
                  pl.BlockSpec((d, LANES), lambda i: (0, 0)),
                  pl.BlockSpec((1, LANES), lambda i: (0, 0))],
        out_specs=[pl.BlockSpec((tm, d + LANES), lambda i: (i, 0)),
                   pl.BlockSpec((tm, LANES), lambda i: (i, 0)),
                   pl.BlockSpec((1, LANES), lambda i: (0, 0))],
        out_shape=[jax.ShapeDtypeStruct((t, d + LANES), BF16),
                   jax.ShapeDtypeStruct((t, LANES), jnp.int32),
                   jax.ShapeDtypeStruct((1, LANES), F32)],
        scratch_shapes=[pltpu.VMEM((1, LANES), F32), pltpu.VMEM((tm, tm), BF16)],
        compiler_params=_cparams("arbitrary"),
        name="moe_router",
    )(x, g.reshape(1, d), w_hi, w_lo, b)
    return tn, route, counts[0, :n_groups].astype(jnp.int32)


def _group_ffn_kernel(tg_ref, first_ref, nu_ref, xs_ref, wg_ref, wu_ref, wd_ref, ys_ref,
                      wg_scr, wu_scr, wd_scr, *, d, per_group):
    i = pl.program_id(0)

    @pl.when(first_ref[i] == 1)
    def _():
        wg_scr[...] = wg_ref[0].astype(BF16)
        wu_scr[...] = wu_ref[0].astype(BF16)
        wd_scr[...] = wd_ref[0].astype(BF16)

    @pl.when(i < nu_ref[0])
    def _():
        x = xs_ref[:, :d]
        pair = xs_ref[:, d:].astype(F32)
        cw = pair[:, :per_group] + pair[:, per_group:2 * per_group]
        acc = None
        for e in range(per_group):
            h = _silu(_dot(x, wg_scr[e])) * _dot(x, wu_scr[e])
            y = _dot((h * cw[:, e:e + 1]).astype(BF16), wd_scr[e])
            acc = y if acc is None else acc + y
        ys_ref[...] = acc

    @pl.when(i >= nu_ref[0])
    def _():
        ys_ref[...] = jnp.zeros_like(ys_ref)


def group_ffn(tile_group, first, n_used, xs, w_gate, w_up, w_down, layer, per_group, tile):
    n_slots, da = xs.shape
    d = da - LANES
    ff = w_gate.shape[-1]
    once = pl.Buffered(1)
    grid_spec = pltpu.PrefetchScalarGridSpec(
        num_scalar_prefetch=3,
        grid=(n_slots // tile,),
        in_specs=[pl.BlockSpec((tile, da), lambda i, tg, fi, nu: (i, 0)),
                  pl.BlockSpec((1, per_group, d, ff), lambda i, tg, fi, nu: (layer, tg[i], 0, 0),
                               pipeline_mode=once),
                  pl.BlockSpec((1, per_group, d, ff), lambda i, tg, fi, nu: (layer, tg[i], 0, 0),
                               pipeline_mode=once),
                  pl.BlockSpec((1, per_group, ff, d), lambda i, tg, fi, nu: (layer, tg[i], 0, 0),
                               pipeline_mode=once)],
        out_specs=pl.BlockSpec((tile, d), lambda i, tg, fi, nu: (i, 0)),
        scratch_shapes=[pltpu.VMEM((per_group, d, ff), BF16),
                        pltpu.VMEM((per_group, d, ff), BF16),
                        pltpu.VMEM((per_group, ff, d), BF16)],
    )
    return pl.pallas_call(
        functools.partial(_group_ffn_kernel, d=d, per_group=per_group),
        grid_spec=grid_spec,
        out_shape=jax.ShapeDtypeStruct((n_slots, d), F32),
        compiler_params=_cparams("arbitrary"),
        name="group_ffn",
    )(tile_group, first, n_used, xs, w_gate, w_up, w_down)


def _combine_kernel(x_ref, y_ref, g_ref, o_ref):
    o_ref[...] = _rms(x_ref[...] + y_ref[...], g_ref[...])


def moe_combine(x, y, g_final):
    t, d = x.shape
    tm = _token_tile(t)
    row = pl.BlockSpec((tm, d), lambda i: (i, 0))
    return pl.pallas_call(
        _combine_kernel,
        grid=(t // tm,),
        in_specs=[row, row, pl.BlockSpec((1, d), lambda i: (0, 0))],
        out_specs=row,
        out_shape=jax.ShapeDtypeStruct((t, d), F32),
        compiler_params=_cparams("parallel"),
        name="moe_combine",
    )(x, y, g_final.reshape(1, d))


def hier_moe(x, g_ffn, w_rg, b_rg, w_re, b_re, w_gate, w_up, w_down, layer, g_final):
    t, d = x.shape
    n_groups = w_rg.shape[1]
    per_group = w_re.shape[1] // n_groups
    tile = 256 if t >= 8192 else 64
    tn, route, counts = moe_router(x, g_ffn, w_rg, b_rg, w_re, b_re)
    grp, rank_t = route[:, 0], route[:, 1]
    groups = jnp.arange(n_groups, dtype=jnp.int32)
    starts = jnp.cumsum(counts) - counts
    padded = (counts + tile - 1) // tile * tile
    pend = jnp.cumsum(padded)
    pstart = pend - padded
    n_tiles = (t + n_groups * (tile - 1)) // tile
    n_slots = n_tiles * tile
    tile_start = jnp.arange(n_tiles, dtype=jnp.int32) * tile
    tile_group = jnp.minimum(jnp.sum(pend[None, :] <= tile_start[:, None], axis=1), n_groups - 1).astype(jnp.int32)
    first = jnp.concatenate([jnp.ones((1,), jnp.int32), (tile_group[1:] != tile_group[:-1]).astype(jnp.int32)])
    n_used = (pend[-1:] // tile).astype(jnp.int32)
    slot_of_token = jnp.sum(jnp.where(grp[:, None] == groups, pstart, 0), axis=-1) + rank_t

    order = jnp.argsort(grp)
    per_slot = lambda v: jnp.repeat(v[tile_group], tile)
    rank_s = jnp.arange(n_slots, dtype=jnp.int32) - per_slot(pstart)
    sorted_pos = jnp.minimum(per_slot(starts) + rank_s, t - 1)
    src = jnp.where(rank_s < per_slot(counts), order.at[sorted_pos].get(mode="promise_in_bounds"), 0)

    xs = tn.at[src].get(mode="promise_in_bounds")
    ys = group_ffn(tile_group, first, n_used, xs, w_gate, w_up, w_down, layer, per_group, tile)
    y = ys.at[slot_of_token].get(mode="promise_in_bounds")
    if g_final is None:
        return y
    return moe_combine(x, y, g_final)


def _latent_q_kernel(x_ref, y_ref, gkv_ref, wdkv_ref, glat_ref, cos_ref, sin_ref, gmix_ref, winb_ref, gq_ref,
                     wuq_ref, wuk_ref, xo_ref, c_ref, kr_ref, cb_ref, krb_ref, qm_ref, ql_ref, qr_ref,
                     *, heads, kv, rd, nope, q_lora, scale):
    x = x_ref[...] + y_ref[...]
    xo_ref[...] = x
    xr = x * lax.rsqrt(jnp.mean(x * x, axis=-1, keepdims=True) + EPS)
    cos = cos_ref[...]
    sin = sin_ref[...]

    ckr = _dot((xr * gkv_ref[...]).astype(BF16), wdkv_ref[...])
    c = _rms(ckr[:, :kv], glat_ref[...])
    kr = ckr[:, kv:kv + rd] * cos + ckr[:, kv + LANES:kv + LANES + rd] * sin
    c_ref[...] = c
    kr_ref[...] = kr
    cb_ref[...] = c.astype(BF16)
    krb_ref[...] = kr.astype(BF16)

    proj = _dot((xr * gmix_ref[...]).astype(BF16), winb_ref[...])
    qm_ref[...] = proj[:, q_lora:]
    cq = _rms(proj[:, :q_lora], gq_ref[...]).astype(BF16)
    q = _dot(cq, wuq_ref[...])
    hw = nope + 2 * LANES
    for h in range(heads):
        qn = q[:, h * hw:h * hw + nope].astype(BF16)
        qrope = (q[:, h * hw + nope:h * hw + nope + rd] * cos
                 + q[:, h * hw + nope + LANES:h * hw + nope + LANES + rd] * sin)
        ql_ref[h] = (_dot(qn, wuk_ref[h]) * scale).astype(BF16)
        qr_ref[h] = (qrope * scale).astype(BF16)


def _swap_halves(w):
    half = w.shape[-1] // 2
    return jnp.concatenate([w[..., half:], w[..., :half]], axis=-1)


def latent_and_queries(x, y, g_kv_in, w_dkv, g_lat, cos_full, sin_signed, g_mix, w_in_b, g_q, w_uq, w_uk,
                       scale, table_blocks):
    t, d = x.shape
    kv = g_lat.shape[0]
    rd = w_dkv.shape[1] - kv
    _, heads, nope = w_uk.shape
    q_lora = g_q.shape[0]
    mem_w = w_in_b.shape[1] - q_lora
    assert rd <= LANES and kv % LANES == 0 and nope % LANES == 0 and q_lora % LANES == 0
    padl = lambda w: jnp.pad(w, ((0, 0), (0, LANES - rd)))
    w_r = w_dkv[:, kv:]
    wdkv_ext = jnp.concatenate([w_dkv[:, :kv], padl(w_r), padl(_swap_halves(w_r))], axis=1).astype(BF16)
    w_uq_h = w_uq.reshape(q_lora, heads, nope + rd)
    w_qr = w_uq_h[..., nope:]
    pad3 = lambda w: jnp.pad(w, ((0, 0), (0, 0), (0, LANES - rd)))
    wuq_ext = jnp.concatenate([w_uq_h[..., :nope], pad3(w_qr), pad3(_swap_halves(w_qr))], axis=-1)
    hw = nope + 2 * LANES
    wuq_ext = wuq_ext.reshape(q_lora, heads * hw).astype(BF16)
    wuk_t = jnp.transpose(w_uk, (1, 2, 0)).astype(BF16)
    tm = _token_tile(t, 256)
    nblk = table_blocks(tm)
    kern = functools.partial(_latent_q_kernel, heads=heads, kv=kv, rd=rd, nope=nope, q_lora=q_lora,
                             scale=scale)
    full = lambda shape: pl.BlockSpec(shape, lambda i: (0,) * len(shape))
    return pl.pallas_call(
        kern,
        grid=(t // tm,),
        in_specs=[pl.BlockSpec((tm, d), lambda i: (i, 0)),
                  pl.BlockSpec((tm, d), lambda i: (i, 0)),
                  full((1, d)), full(wdkv_ext.shape), full((1, kv)),
                  pl.BlockSpec((tm, rd), lambda i: (i % nblk, 0)),
                  pl.BlockSpec((tm, rd), lambda i: (i % nblk, 0)),
                  full((1, d)), full(w_in_b.shape), full((1, q_lora)),
                  full(wuq_ext.shape), full(wuk_t.shape)],
        out_specs=[pl.BlockSpec((tm, d), lambda i: (i, 0)),
                   pl.BlockSpec((tm, kv), lambda i: (i, 0)),
                   pl.BlockSpec((tm, rd), lambda i: (i, 0)),
                   pl.BlockSpec((tm, kv), lambda i: (i, 0)),
                   pl.BlockSpec((tm, rd), lambda i: (i, 0)),
                   pl.BlockSpec((tm, mem_w), lambda i: (i, 0)),
                   pl.BlockSpec((heads, tm, kv), lambda i: (0, i, 0)),
                   pl.BlockSpec((heads, tm, rd), lambda i: (0, i, 0))],
        out_shape=[jax.ShapeDtypeStruct((t, d), F32),
                   jax.ShapeDtypeStruct((t, kv), F32),
                   jax.ShapeDtypeStruct((t, rd), F32),
                   jax.ShapeDtypeStruct((t, kv), BF16),
                   jax.ShapeDtypeStruct((t, rd), BF16),
                   jax.ShapeDtypeStruct((t, mem_w), F32),
                   jax.ShapeDtypeStruct((heads, t, kv), BF16),
                   jax.ShapeDtypeStruct((heads, t, rd), BF16)],
        compiler_params=_cparams("parallel"),
        name="latent_and_queries",
    )(x, y, g_kv_in.reshape(1, d), wdkv_ext, g_lat.reshape(1, kv), cos_full, sin_signed,
      g_mix.reshape(1, d), w_in_b.astype(BF16), g_q.reshape(1, q_lora), wuq_ext, wuk_t)


def _mla_prompt_kernel(qi_ref, kj_ref, ql_ref, qr_ref, c_ref, kr_ref, wuv_ref, o_ref,
                       m_scr, l_scr, acc_scr, *, heads, tq, vh, sub):
    p_idx = pl.program_id(1)
    qi = qi_ref[p_idx]
    kj = kj_ref[p_idx]
    kv = c_ref.shape[-1]

    @pl.when(kj == 0)
    def _():
        m_scr[...] = jnp.full_like(m_scr, NEG_INF)
        l_scr[...] = jnp.zeros_like(l_scr)
        acc_scr[...] = jnp.zeros_like(acc_scr)

    def step(masked):
        c = c_ref[...]
        kr = kr_ref[...]
        for h in range(heads):
            for r0 in range(0, tq, sub):
                rows = slice(h * tq + r0, h * tq + r0 + sub)
                s = _dot_nt(ql_ref[h, r0:r0 + sub, :], c) + _dot_nt(qr_ref[h, r0:r0 + sub, :], kr)
                if masked:
                    qpos = r0 + lax.broadcasted_iota(jnp.int32, (sub, tq), 0)
                    kpos = lax.broadcasted_iota(jnp.int32, (sub, tq), 1)
                    s = jnp.where(kpos <= qpos, s, NEG_INF)
                m_prev = m_scr[rows]
                m_new = jnp.maximum(m_prev, jnp.max(s, axis=-1, keepdims=True))
                alpha = jnp.exp(m_prev - m_new)
                p = jnp.exp(s - _lane_tile(m_new, tq // LANES))
                l_scr[rows] = alpha * l_scr[rows] + jnp.sum(p, axis=-1, keepdims=True)
                acc_scr[rows] = _lane_tile(alpha, kv // LANES) * acc_scr[rows] + _dot(p.astype(BF16), c)
                m_scr[rows] = m_new

    @pl.when(kj < qi)
    def _():
        step(False)

    @pl.when(kj == qi)
    def _():
        step(True)
        for h in range(heads):
            rows = slice(h * tq, (h + 1) * tq)
            out = acc_scr[rows] / _lane_tile(l_scr[rows], kv // LANES)
            o_ref[:, h * vh:(h + 1) * vh] = _dot(out.astype(BF16), wuv_ref[h])


def mla_prompt_attention(ql, qr, cb, krb, w_uv, bsz, seq, tq=512, sub=512):
    heads, t, kv = ql.shape
    rd = qr.shape[-1]
    vh = w_uv.shape[-1]
    tq = min(tq, seq)
    assert seq % tq == 0
    nq = seq // tq
    pairs = [(i, j) for i in range(nq) for j in range(i + 1)]
    qi = jnp.asarray([p[0] for p in pairs], jnp.int32)
    kj = jnp.asarray([p[1] for p in pairs], jnp.int32)
    wuv_h = jnp.transpose(w_uv, (1, 0, 2)).astype(BF16)
    grid_spec = pltpu.PrefetchScalarGridSpec(
        num_scalar_prefetch=2,
        grid=(bsz, len(pairs)),
        in_specs=[pl.BlockSpec((heads, tq, kv), lambda b, p, qi, kj: (0, b * nq + qi[p], 0)),
                  pl.BlockSpec((heads, tq, rd), lambda b, p, qi, kj: (0, b * nq + qi[p], 0)),
                  pl.BlockSpec((tq, kv), lambda b, p, qi, kj: (b * nq + kj[p], 0)),
                  pl.BlockSpec((tq, rd), lambda b, p, qi, kj: (b * nq + kj[p], 0)),
                  pl.BlockSpec((heads, kv, vh), lambda b, p, qi, kj: (0, 0, 0))],
        out_specs=pl.BlockSpec((tq, heads * vh), lambda b, p, qi, kj: (b * nq + qi[p], 0)),
        scratch_shapes=[pltpu.VMEM((heads * tq, LANES), F32),
                        pltpu.VMEM((heads * tq, LANES), F32),
                        pltpu.VMEM((heads * tq, kv), F32)],
    )
    sub = min(sub, tq)
    assert tq % sub == 0 and tq % LANES == 0 and kv % LANES == 0
    return pl.pallas_call(
        functools.partial(_mla_prompt_kernel, heads=heads, tq=tq, vh=vh, sub=sub),
        grid_spec=grid_spec,
        out_shape=jax.ShapeDtypeStruct((t, heads * vh), F32),
        compiler_params=_cparams("parallel", "arbitrary"),
        name="mla_prompt_attention",
    )(qi, kj, ql, qr, cb, krb, wuv_h)


def _mla_paged_kernel(pt_ref, ql_ref, qr_ref, cn_ref, krn_ref, clat_hbm, ckr_hbm, wuv_ref, o_ref,
                      kbuf, rbuf, sems, m_scr, l_scr, acc_scr,
                      *, heads, n_new, head_slots, page, pages_per_chunk, chunks_per_seq, sub_rows, vh):
    g = pl.program_id(0)
    total = pl.num_programs(0)
    ci = g % chunks_per_seq
    slot = g % 2

    def page_copies(chunk, slot_, p):
        pg = pt_ref[chunk * pages_per_chunk + p]
        rows = pl.ds(p * page, page)
        return (pltpu.make_async_copy(clat_hbm.at[pg], kbuf.at[slot_, rows], sems.at[0, slot_]),
                pltpu.make_async_copy(ckr_hbm.at[pg], rbuf.at[slot_, :, rows], sems.at[1, slot_]))

    def start_chunk(chunk, slot_):
        for p in range(pages_per_chunk):
            for cp in page_copies(chunk, slot_, p):
                cp.start()

    @pl.when(g == 0)
    def _():
        start_chunk(0, 0)

    @pl.when(g + 1 < total)
    def _():
        start_chunk(g + 1, 1 - slot)

    for p in range(pages_per_chunk):
        for cp in page_copies(g, slot, p):
            cp.wait()

    @pl.when(ci == 0)
    def _():
        m_scr[...] = jnp.full_like(m_scr, NEG_INF)
        l_scr[...] = jnp.zeros_like(l_scr)
        acc_scr[...] = jnp.zeros_like(acc_scr)

    ql = ql_ref[0]
    qr = qr_ref[0]

    def partial_softmax(s, v):
        m = jnp.max(s, axis=-1, keepdims=True)
        p = jnp.exp(s - m)
        return m, jnp.sum(p, axis=-1, keepdims=True), _dot(p.astype(BF16), v)

    def merge(parts):
        m_prev = m_scr[...]
        m_new = m_prev
        for m, _, _ in parts:
            m_new = jnp.maximum(m_new, m)
        alpha = jnp.exp(m_prev - m_new)
        l_new = alpha * l_scr[...]
        acc = alpha * acc_scr[...]
        for m, l_part, a_part in parts:
            w = jnp.exp(m - m_new)
            l_new = l_new + w * l_part
            acc = acc + w * a_part
        m_scr[...] = m_new
        l_scr[...] = l_new
        acc_scr[...] = acc

    parts = []
    for r0 in range(0, pages_per_chunk * page, sub_rows):
        k = kbuf[slot, r0:r0 + sub_rows, :].astype(BF16)
        kr_t = rbuf[slot, :, r0:r0 + sub_rows].astype(BF16)
        parts.append(partial_softmax(_dot_nt(ql, k) + _dot(qr, kr_t), k))
    merge(parts)

    @pl.when(ci == chunks_per_seq - 1)
    def _():
        cn = cn_ref[0].astype(BF16)
        s = _dot_nt(ql, cn) + _dot_nt(qr, krn_ref[0].astype(BF16))
        qpos = lax.broadcasted_iota(jnp.int32, s.shape, 0) // head_slots
        kpos = lax.broadcasted_iota(jnp.int32, s.shape, 1)
        merge([partial_softmax(jnp.where(kpos <= qpos, s, NEG_INF), cn)])
        out = (acc_scr[...] / l_scr[...]).astype(BF16)
        full = jnp.concatenate([_dot(out, wuv_ref[h]) for h in range(heads)], axis=-1)
        row_head = lax.broadcasted_iota(jnp.int32, full.shape, 0) % head_slots
        lane_head = lax.broadcasted_iota(jnp.int32, full.shape, 1) // vh
        own = jnp.where(row_head == lane_head, full, 0.0)
        o_ref[0] = jnp.sum(own.reshape(n_new, head_slots, heads * vh), axis=1)


def mla_paged_attention(page_table, ql, qr, c_new, kr_new, cache_lat, cache_kr, w_uv, head_slots):
    bsz, n_pages = page_table.shape
    _, page, kv = cache_lat.shape
    rd = cache_kr.shape[-1]
    q_rows = ql.shape[1]
    n_new = c_new.shape[1]
    heads, vh = w_uv.shape[1], w_uv.shape[2]
    pages_per_chunk = min(n_pages, 64)
    assert n_pages % pages_per_chunk == 0
    chunks_per_seq = n_pages // pages_per_chunk
    chunk_rows = pages_per_chunk * page
    sub_rows = min(chunk_rows, 1024)
    assert chunk_rows % sub_rows == 0
    new_pad = -n_new % SUBLANES
    c_new = jnp.pad(c_new, ((0, 0), (0, new_pad), (0, 0)))
    kr_new = jnp.pad(kr_new, ((0, 0), (0, new_pad), (0, 0)))
    wuv_h = jnp.transpose(w_uv, (1, 0, 2)).astype(BF16)
    cache_kr_t = jnp.transpose(cache_kr, (0, 2, 1))
    assert q_rows == n_new * head_slots and head_slots % SUBLANES == 0 and heads <= head_slots
    kern = functools.partial(_mla_paged_kernel, heads=heads, n_new=n_new, head_slots=head_slots, page=page,
                             pages_per_chunk=pages_per_chunk, chunks_per_seq=chunks_per_seq,
                             sub_rows=sub_rows, vh=vh)
    seq_of = lambda g, pt: (g // chunks_per_seq, 0, 0)
    grid_spec = pltpu.PrefetchScalarGridSpec(
        num_scalar_prefetch=1,
        grid=(bsz * chunks_per_seq,),
        in_specs=[pl.BlockSpec((1, q_rows, kv), seq_of),
                  pl.BlockSpec((1, q_rows, rd), seq_of),
                  pl.BlockSpec((1, n_new + new_pad, kv), seq_of),
                  pl.BlockSpec((1, n_new + new_pad, rd), seq_of),
                  pl.BlockSpec(memory_space=pl.ANY),
                  pl.BlockSpec(memory_space=pl.ANY),
                  pl.BlockSpec((heads, kv, vh), lambda g, pt: (0, 0, 0))],
        out_specs=pl.BlockSpec((1, n_new, heads * vh), seq_of),
        scratch_shapes=[pltpu.VMEM((2, chunk_rows, kv), F32),
                        pltpu.VMEM((2, rd, chunk_rows), F32),
                        pltpu.SemaphoreType.DMA((2, 2)),
                        pltpu.VMEM((q_rows, 1), F32),
                        pltpu.VMEM((q_rows, 1), F32),
                        pltpu.VMEM((q_rows, kv), F32)],
    )
    return pl.pallas_call(
        kern,
        grid_spec=grid_spec,
        out_shape=jax.ShapeDtypeStruct((bsz, n_new, heads * vh), F32),
        compiler_params=_cparams("arbitrary"),
        name="mla_paged_attention",
    )(page_table.reshape(-1), ql, qr, c_new, kr_new, cache_lat, cache_kr_t, wuv_h)


def _rope_tables(pos, rd):
    inv = ROPE_THETA ** (-jnp.arange(0, rd, 2, dtype=F32) / rd)
    ang = pos.astype(F32)[:, None] * inv[None, :]
    cos, sin = jnp.cos(ang), jnp.sin(ang)
    return jnp.concatenate([cos, cos], axis=-1), jnp.concatenate([-sin, sin], axis=-1)


def _run_group(x3, pos, mem, conv_buf, d_state, paged, p):
    bsz, seq, d = x3.shape
    t = bsz * seq
    x = x3.reshape(t, d)
    heads_dn, dk, dv = d_state.shape[1:]
    conv_ch = p["conv_w"].shape[-1]
    v_w = heads_dn * dv
    mem_k, mem_v, mem_heads, mem_feature_major = mem
    mem_w = mem_k[0].shape[1] if mem_feature_major else mem_k[0].shape[2]

    w_in = p["w_in_a"][0]
    w_ab = jnp.pad(w_in[:, conv_ch + v_w:conv_ch + v_w + 2 * heads_dn], ((0, 0), (0, LANES - 2 * heads_dn)))
    w_in_r = jnp.concatenate([w_in[:, :conv_ch + v_w], w_in[:, conv_ch + v_w + 2 * heads_dn:], w_ab],
                             axis=1).astype(BF16)
    qkv, z, qm, ab = norm_matmul(x, p["g_mix"][0], w_in_r, (conv_ch, v_w, mem_w, LANES), (F32,) * 4)
    rows = DELTA_CHUNK if seq % DELTA_CHUNK == 0 else -(-seq // SUBLANES) * SUBLANES
    seq_p = -(-seq // rows) * rows
    seq3 = lambda a: jnp.pad(a.reshape(bsz, seq, -1), ((0, 0), (0, seq_p - seq), (0, 0)))
    o_dn, new_buf, new_state = delta_mixer(seq3(qkv), seq3(ab), seq3(z), conv_buf, d_state, p["conv_w"][0],
                                           p["a_log"][0], p["dt_bias"][0], p["g_onorm"][0],
                                           rows=rows, n_valid=min(seq, rows),
                                           nb=math.gcd(bsz, 2 if rows >= DELTA_CHUNK else 4))
    o_dn = o_dn[:, :seq].reshape(t, v_w)
    mo = mem_attention(seq3(qm), mem_k[0], mem_v[0], mem_heads, mem_feature_major)[:, :seq].reshape(t, mem_w)
    x = proj_residual(o_dn, mo, p["w_out_a"][0].astype(BF16), x)
    y_moe = hier_moe(x, p["g_ffn"][0], p["w_router_group"][0], p["b_router_group"][0], p["w_router_expert"][0],
                     p["b_router_expert"][0], p["w_exp_gate"], p["w_exp_up"], p["w_exp_down"], 0, None)

    kv = p["g_kv_latent"].shape[0]
    rd = p["w_dkv"].shape[1] - kv
    nope = p["w_uk"].shape[2]
    mla_heads = p["w_uk"].shape[1]
    scale = (nope + rd) ** -0.5
    if paged is None:
        cos_t, sin_t = _rope_tables(pos, rd)
        table_blocks = lambda tm: seq // tm
    else:
        cos_t, sin_t = _rope_tables(jnp.tile(pos, bsz), rd)
        table_blocks = lambda tm: t // tm
    x, c, kr, cb, krb, qm, ql, qr = latent_and_queries(
        x, y_moe, p["g_kv_in"], p["w_dkv"], p["g_kv_latent"], cos_t, sin_t, p["g_mix"][1], p["w_in_b"][0],
        p["g_q"][0], p["w_uq"][0], p["w_uk"], scale, table_blocks)
    if paged is None:
        o_mla = mla_prompt_attention(ql, qr, cb, krb, p["w_uv"], bsz, seq)
    else:
        head_slots = -(-mla_heads // SUBLANES) * SUBLANES
        to_rows = lambda q: jnp.pad(
            jnp.transpose(q.reshape(mla_heads, bsz, seq, -1), (1, 2, 0, 3)),
            ((0, 0), (0, 0), (0, head_slots - mla_heads), (0, 0))).reshape(bsz, seq * head_slots, -1)
        o_mla = mla_paged_attention(paged[2], to_rows(ql), to_rows(qr), c.reshape(bsz, seq, kv),
                                    kr.reshape(bsz, seq, rd), paged[0], paged[1], p["w_uv"],
                                    head_slots).reshape(t, -1)
    mo = mem_attention(seq3(qm), mem_k[1], mem_v[1], mem_heads, mem_feature_major)[:, :seq].reshape(t, mem_w)
    x = proj_residual(o_mla, mo, p["w_out_b"][0].astype(BF16), x)
    y = hier_moe(x, p["g_ffn"][1], p["w_router_group"][1], p["b_router_group"][1], p["w_router_expert"][1],
                 p["b_router_expert"][1], p["w_exp_gate"], p["w_exp_up"], p["w_exp_down"], 1, p["g_final"])
    return (y.reshape(bsz, seq, d), c.reshape(bsz, seq, kv), kr.reshape(bsz, seq, rd),
            new_buf[None], new_state[None])


def kernel(x_prompt, x_sample, mem_prompt, state_delta, cache_conv, cache_kv_latent, cache_k_rope, cache_mem_k, cache_mem_v, page_table, g_mix, g_ffn, g_final, w_in_a, conv_w, a_log, dt_bias, g_onorm, w_out_a, g_kv_in, w_dkv, g_kv_latent, w_uk, w_uv, w_in_b, g_q, w_uq, w_out_b, g_mem, w_mem_k, w_mem_v, w_router_group, b_router_group, w_router_expert, b_router_expert, w_exp_gate, w_exp_up, w_exp_down):
    p = dict(g_mix=g_mix, g_ffn=g_ffn, g_final=g_final, w_in_a=w_in_a, conv_w=conv_w, a_log=a_log,
             dt_bias=dt_bias, g_onorm=g_onorm, w_out_a=w_out_a, g_kv_in=g_kv_in, w_dkv=w_dkv,
             g_kv_latent=g_kv_latent, w_uk=w_uk, w_uv=w_uv, w_in_b=w_in_b, g_q=g_q, w_uq=w_uq,
             w_out_b=w_out_b, w_router_group=w_router_group, b_router_group=b_router_group,
             w_router_expert=w_router_expert, b_router_expert=b_router_expert, w_exp_gate=w_exp_gate,
             w_exp_up=w_exp_up, w_exp_down=w_exp_down)
    depth = g_mix.shape[0]
    assert depth == 2 and w_in_a.shape[0] == 1
    bp, lp, d = x_prompt.shape
    bs, ls, _ = x_sample.shape
    mem_tokens = mem_prompt.shape[1]
    mem_heads, mem_hd = cache_mem_k.shape[3], cache_mem_k.shape[4]
    mem_w = mem_heads * mem_hd

    mem_flat = mem_prompt.reshape(bp * mem_tokens, d)
    mkv = [norm_matmul(mem_flat, g_mem[l], jnp.concatenate([w_mem_k[l], w_mem_v[l]], axis=1).astype(BF16),
                       (mem_w, mem_w), (F32, F32)) for l in range(depth)]
    shape5 = (bp, mem_tokens, mem_heads, mem_hd)
    mem_k_prompt = jnp.stack([m[0].reshape(shape5) for m in mkv])
    mem_v_prompt = jnp.stack([m[1].reshape(shape5) for m in mkv])
    conv0 = jnp.zeros((bp,) + cache_conv.shape[2:], F32)
    state0 = jnp.zeros((bp,) + state_delta.shape[2:], F32)
    mem3 = lambda m: m.reshape(bp, mem_tokens, mem_w)
    y_p, c_p, kr_p, conv_p, state_p = _run_group(
        x_prompt, jnp.arange(lp), ([mem3(m[0]) for m in mkv], [mem3(m[1]) for m in mkv], mem_heads, False),
        conv0, state0, None, p)

    past_len = page_table.shape[1] * cache_kv_latent.shape[1]
    feat3 = lambda m: jnp.transpose(m, (0, 2, 3, 1)).reshape(bs, mem_w, mem_tokens)
    y_s, c_s, kr_s, conv_s, state_s = _run_group(
        x_sample, past_len + jnp.arange(ls),
        ([feat3(cache_mem_k[l]) for l in range(depth)], [feat3(cache_mem_v[l]) for l in range(depth)],
         mem_heads, True),
        cache_conv[0], state_delta[0], (cache_kv_latent, cache_k_rope, page_table), p)

    return (y_p, y_s, c_p, kr_p, c_s, kr_s, state_p, conv_p, state_s, conv_s, mem_k_prompt, mem_v_prompt)
```
